```python
import math
import jax, jax.numpy as jnp
from jax import lax
import numpy as np

D_MODEL = 1024
BATCH = 8
SEQ = 4096
DEPTH = 4

GRID_W = 64
CTX_LEN = 256
CHUNK = 64
RMS_EPS = 1e-6
MIX_W = D_MODEL
GLA_W = MIX_W // 4
GLA_HEAD_V = 64
GLA_HEADS = GLA_W // GLA_HEAD_V
GLA_HEAD_K = GLA_HEAD_V // 2
GLA_KDIM = GLA_HEADS * GLA_HEAD_K
GLA_RANK = 16
GLA_GATE_TAU = 16.0
SSD_W = MIX_W // 2
SSD_HEAD_DIM = 64
SSD_HEADS = SSD_W // SSD_HEAD_DIM
SSD_GROUPS = 2
SSD_STATE = 128
SSD_CONV_W = 5
SSD_CONV_CH = SSD_W + 2 * SSD_GROUPS * SSD_STATE
RET_W = MIX_W - GLA_W - SSD_W
RET_HEAD_DIM = 64
RET_HEADS = RET_W // RET_HEAD_DIM
ROPE_BASE = 10000.0
GLA_COLS = 2 * GLA_KDIM + 2 * GLA_W + 2 * GLA_RANK
SSD_COLS = SSD_W + SSD_CONV_CH + 2 * SSD_HEADS
RET_COLS = 4 * RET_W
IN_COLS = GLA_COLS + SSD_COLS + RET_COLS
FFN_HIDDEN = -(-8 * D_MODEL // (3 * 256)) * 256

kernel_name = "hybrid_gla_ssd_retention_dit_block"


def rms_norm(x, w, eps=RMS_EPS):
    xf = x.astype(jnp.float32)
    y = xf * lax.rsqrt(jnp.mean(xf * xf, axis=-1, keepdims=True) + eps)
    return (y * w.astype(jnp.float32)).astype(x.dtype)


def layer_norm(x, w, eps=RMS_EPS):
    xf = x.astype(jnp.float32)
    mu = jnp.mean(xf, axis=-1, keepdims=True)
    xc = xf - mu
    y = xc * lax.rsqrt(jnp.mean(xc * xc, axis=-1, keepdims=True) + eps)
    return (y * w.astype(jnp.float32)).astype(x.dtype)


def modulate(x, shift, scale):
    return x * (1 + scale) + shift


def depthwise_conv(u, w, b):
    pad = (w.shape[0] - 1) // 2
    y = lax.conv_general_dilated(u, w[:, None, :], window_strides=(1,), padding=[(pad, pad)],
                                 dimension_numbers=('NWC', 'WIO', 'NWC'),
                                 feature_group_count=u.shape[-1])
    return y + b


def chunked_scan(q, k, v, logg, s0):
    B, T, H, Dk = q.shape
    Dv = v.shape[-1]
    n = T // CHUNK
    scalar = logg.shape[-1] == 1
    mask = jnp.tril(jnp.ones((CHUNK, CHUNK), dtype=bool))[None, :, :, None, None]

    def split(a):
        return a.reshape(B, n, CHUNK, H, a.shape[-1]).swapaxes(0, 1)

    def step(S, inp):
        qc, kc, vc, gc = inp
        G = jnp.cumsum(gc, axis=1)
        diff = G[:, :, None] - G[:, None, :]
        dec = jnp.where(mask, jnp.exp(jnp.minimum(diff, 0.0)), 0.0)
        if scalar:
            scores = jnp.einsum('bihd,bjhd->bijh', qc, kc) * dec[..., 0]
        else:
            scores = jnp.einsum('bihd,bjhd,bijhd->bijh', qc, kc, dec)
        intra = jnp.einsum('bijh,bjhe->bihe', scores, vc)
        inter = jnp.einsum('bihd,bhde->bihe', qc * jnp.exp(G), S)
        G_last = G[:, -1:]
        S_new = jnp.exp(G_last[:, 0])[..., None] * S + jnp.einsum(
            'bjhd,bjhe->bhde', kc * jnp.exp(G_last - G), vc)
        return S_new, intra + inter

    S, o = lax.scan(step, s0, (split(q), split(k), split(v), split(logg)))
    o = o.swapaxes(0, 1).reshape(B, T, H, Dv)
    return o, S


def bidir_scan(ctx_in, lat_in):
    qc, kfc, kbc, vc, gfc, gbc = ctx_in
    ql, kfl, kbl, vl, gfl, gbl = lat_in
    B, _, H, Dk = qc.shape
    Dv = vc.shape[-1]
    s0 = jnp.zeros((B, H, Dk, Dv), qc.dtype)
    flip = lambda a: jnp.flip(a, axis=1)
    oc_f, S_f = chunked_scan(qc, kfc, vc, gfc, s0)
    ol_f, _ = chunked_scan(ql, kfl, vl, gfl, S_f)
    oc_b, S_b = chunked_scan(flip(qc), flip(kbc), flip(vc), flip(gbc), s0)
    ol_b, _ = chunked_scan(flip(ql), flip(kbl), flip(vl), flip(gbl), S_b)
    return oc_f + flip(oc_b), ol_f + flip(ol_b)


def gla_mixer(p_ctx, p_lat, gate_up, gate_b, norm_w):
    def prep(p):
        B, T, _ = p.shape
        q, k, v, r, lr = jnp.split(p, [GLA_KDIM, 2 * GLA_KDIM, 2 * GLA_KDIM + GLA_W,
                                       2 * GLA_KDIM + 2 * GLA_W], axis=-1)
        q = q.reshape(B, T, GLA_HEADS, GLA_HEAD_K) * GLA_HEAD_K ** -0.5
        k = k.reshape(B, T, GLA_HEADS, GLA_HEAD_K)
        v = v.reshape(B, T, GLA_HEADS, GLA_HEAD_V)
        z = jnp.einsum('btnr,nrk->btnk', lr.reshape(B, T, 2, GLA_RANK), gate_up) + gate_b
        logg = jax.nn.log_sigmoid(z) / GLA_GATE_TAU
        g_f = logg[:, :, 0].reshape(B, T, GLA_HEADS, GLA_HEAD_K)
        g_b = logg[:, :, 1].reshape(B, T, GLA_HEADS, GLA_HEAD_K)
        return (q, k, k, v, g_f, g_b), r

    in_c, r_c = prep(p_ctx)
    in_l, r_l = prep(p_lat)
    o_c, o_l = bidir_scan(in_c, in_l)

    def out(o, r):
        B, T = o.shape[:2]
        o = rms_norm(o, norm_w.reshape(GLA_HEADS, GLA_HEAD_V)).reshape(B, T, GLA_W)
        return o * jax.nn.silu(r)

    return out(o_c, r_c), out(o_l, r_l)


def ssd_mixer(p_ctx, p_lat, conv_w, conv_b, dt_bias, a_log, d_skip, norm_w):
    def prep(p):
        B, T, _ = p.shape
        z, xbc, dt = jnp.split(p, [SSD_W, SSD_W + SSD_CONV_CH], axis=-1)
        xbc = jax.nn.silu(depthwise_conv(xbc, conv_w, conv_b))
        xs, bm, cm = jnp.split(xbc, [SSD_W, SSD_W + SSD_GROUPS * SSD_STATE], axis=-1)
        rep = SSD_HEADS // SSD_GROUPS
        xs = xs.reshape(B, T, SSD_HEADS, SSD_HEAD_DIM)
        bm = jnp.repeat(bm.reshape(B, T, SSD_GROUPS, SSD_STATE), rep, axis=2)
        cm = jnp.repeat(cm.reshape(B, T, SSD_GROUPS, SSD_STATE), rep, axis=2)
        dt = jax.nn.softplus(dt.reshape(B, T, 2, SSD_HEADS) + dt_bias)
        logg = dt * (-jnp.exp(a_log))
        k_f = bm * dt[:, :, 0, :, None]
        k_b = bm * dt[:, :, 1, :, None]
        return (cm, k_f, k_b, xs, logg[:, :, 0, :, None], logg[:, :, 1, :, None]), z, xs

    in_c, z_c, x_c = prep(p_ctx)
    in_l, z_l, x_l = prep(p_lat)
    y_c, y_l = bidir_scan(in_c, in_l)

    def out(y, z, xs):
        B, T = y.shape[:2]
        y = (y + d_skip[:, None] * xs).reshape(B, T, SSD_W)
        return rms_norm(y * jax.nn.silu(z), norm_w)

    return out(y_c, z_c, x_c), out(y_l, z_l, x_l)


def apply_rope(t, cos, sin):
    half = t.shape[-1] // 2
    t1, t2 = t[..., :half], t[..., half:]
    return jnp.concatenate([t1 * cos - t2 * sin, t2 * cos + t1 * sin], axis=-1)


def retention_mixer(p_ctx, p_lat, cos, sin, norm_w):
    log_gamma = jnp.log1p(-jnp.exp2(-5.0 - jnp.arange(RET_HEADS, dtype=jnp.float32)))
    log_gamma = log_gamma.astype(p_lat.dtype)

    def prep(p, rotate):
        B, T, _ = p.shape
        q, k, v, g = jnp.split(p, 4, axis=-1)
        q = q.reshape(B, T, RET_HEADS, RET_HEAD_DIM) * RET_HEAD_DIM ** -0.5
        k = k.reshape(B, T, RET_HEADS, RET_HEAD_DIM)
        v = v.reshape(B, T, RET_HEADS, RET_HEAD_DIM)
        if rotate:
            q = apply_rope(q, cos, sin)
            k = apply_rope(k, cos, sin)
        lg = jnp.broadcast_to(log_gamma[:, None], (B, T, RET_HEADS, 1))
        return (q, k, k, v, lg, lg), g

    in_c, g_c = prep(p_ctx, False)
    in_l, g_l = prep(p_lat, True)
    o_c, o_l = bidir_scan(in_c, in_l)

    def out(o, g):
        B, T = o.shape[:2]
        o = layer_norm(o, norm_w.reshape(RET_HEADS, RET_HEAD_DIM)).reshape(B, T, RET_W)
        return o * jax.nn.silu(g)

    return out(o_c, g_c), out(o_l, g_l)


def swiglu(h, w13, w2):
    gate, up = jnp.split(h @ w13, 2, axis=-1)
    return (jax.nn.silu(gate) * up) @ w2


def _fwd_setup_inputs(seed: int = 0) -> dict:
    key = jax.random.key(seed)
    ks = jax.random.split(key, 24)
    f32 = jnp.float32
    nrm = lambda k, shape, s: jax.random.normal(k, shape, f32) * s
    gain = lambda k, shape: 1.0 + 0.05 * jax.random.normal(k, shape, f32)
    D = D_MODEL
    dt = jnp.exp(jax.random.uniform(ks[17], (DEPTH, 2, SSD_HEADS), f32)
                 * (math.log(0.1) - math.log(0.001)) + math.log(0.001))
    return {
        "x": nrm(ks[0], (BATCH, SEQ, D), 1.0),
        "c": nrm(ks[1], (BATCH, D), 1.0),
        "ctx": nrm(ks[2], (BATCH, CTX_LEN, D), 1.0),
        "c_ctx": nrm(ks[3], (D,), 1.0),
        "ada_w": nrm(ks[4], (DEPTH, D, 6 * D), D ** -0.5),
        "ada_b": nrm(ks[5], (DEPTH, 6 * D), 0.02),
        "norm_mix_pre": gain(ks[6], (DEPTH, D)),
        "norm_mix_post": gain(ks[7], (DEPTH, D)),
        "norm_ffn_pre": gain(ks[8], (DEPTH, D)),
        "norm_ffn_post": gain(ks[9], (DEPTH, D)),
        "w_in": nrm(ks[10], (DEPTH, D, IN_COLS), D ** -0.5),
        "w_out": nrm(ks[11], (DEPTH, MIX_W, D), MIX_W ** -0.5),
        "gla_gate_up": nrm(ks[12], (DEPTH, 2, GLA_RANK, GLA_KDIM), GLA_RANK ** -0.5),
        "gla_gate_b": nrm(ks[13], (DEPTH, 2, GLA_KDIM), 0.1),
        "gla_norm": gain(ks[14], (DEPTH, GLA_W)),
        "ssd_conv_w": nrm(ks[15], (DEPTH, SSD_CONV_W, SSD_CONV_CH), SSD_CONV_W ** -0.5),
        "ssd_conv_b": nrm(ks[16], (DEPTH, SSD_CONV_CH), 0.02),
        "ssd_dt_bias": dt + jnp.log(-jnp.expm1(-dt)),
        "ssd_a_log": jnp.log(jax.random.uniform(ks[18], (DEPTH, 2, SSD_HEADS), f32, 1.0, 16.0)),
        "ssd_d": gain(ks[19], (DEPTH, SSD_HEADS)),
        "ssd_norm": gain(ks[20], (DEPTH, SSD_W)),
        "ret_norm": gain(ks[21], (DEPTH, RET_W)),
        "ffn_w13": nrm(ks[22], (DEPTH, D, 2 * FFN_HIDDEN), D ** -0.5),
        "ffn_w2": nrm(ks[23], (DEPTH, FFN_HIDDEN, D), FFN_HIDDEN ** -0.5),
    }


def _fwd_reference(x, c, ctx, c_ctx, ada_w, ada_b, norm_mix_pre, norm_mix_post, norm_ffn_pre,
              norm_ffn_post, w_in, w_out, gla_gate_up, gla_gate_b, gla_norm, ssd_conv_w,
              ssd_conv_b, ssd_dt_bias, ssd_a_log, ssd_d, ssd_norm, ret_norm, ffn_w13, ffn_w2):
    T = x.shape[1]
    rows = T // GRID_W
    row = jnp.repeat(jnp.arange(rows), GRID_W).astype(jnp.float32)
    col = jnp.tile(jnp.arange(GRID_W), rows).astype(jnp.float32)
    n_freq = RET_HEAD_DIM // 4
    inv_freq = ROPE_BASE ** (-jnp.arange(n_freq, dtype=jnp.float32) / n_freq)
    ang = jnp.concatenate([row[:, None] * inv_freq, col[:, None] * inv_freq], axis=-1)
    cos = jnp.cos(ang).astype(x.dtype)[None, :, None, :]
    sin = jnp.sin(ang).astype(x.dtype)[None, :, None, :]

    lat, cx = x, ctx
    s1, s2 = GLA_COLS, GLA_COLS + SSD_COLS
    for l in range(DEPTH):
        last = l == DEPTH - 1
        mod_l = (jax.nn.silu(c) @ ada_w[l] + ada_b[l])[:, None, :]
        mod_c = (jax.nn.silu(c_ctx) @ ada_w[l] + ada_b[l])[None, None, :]
        sh1, sc1, gt1, sh2, sc2, gt2 = jnp.split(mod_l, 6, axis=-1)
        csh1, csc1, cgt1, csh2, csc2, cgt2 = jnp.split(mod_c, 6, axis=-1)

        p_l = modulate(rms_norm(lat, norm_mix_pre[l]), sh1, sc1) @ w_in[l]
        p_c = modulate(rms_norm(cx, norm_mix_pre[l]), csh1, csc1) @ w_in[l]
        gla_c, gla_l = gla_mixer(p_c[..., :s1], p_l[..., :s1],
                                 gla_gate_up[l], gla_gate_b[l], gla_norm[l])
        ssd_c, ssd_l = ssd_mixer(p_c[..., s1:s2], p_l[..., s1:s2], ssd_conv_w[l], ssd_conv_b[l],
                                 ssd_dt_bias[l], ssd_a_log[l], ssd_d[l], ssd_norm[l])
        ret_c, ret_l = retention_mixer(p_c[..., s2:], p_l[..., s2:], cos, sin, ret_norm[l])
        mixed_l = jnp.concatenate([gla_l, ssd_l, ret_l], axis=-1) @ w_out[l]
        lat = lat + gt1 * rms_norm(mixed_l, norm_mix_post[l])

        h_l = modulate(rms_norm(lat, norm_ffn_pre[l]), sh2, sc2)
        lat = lat + gt2 * rms_norm(swiglu(h_l, ffn_w13[l], ffn_w2[l]), norm_ffn_post[l])

        if not last:
            mixed_c = jnp.concatenate([gla_c, ssd_c, ret_c], axis=-1) @ w_out[l]
            cx = cx + cgt1 * rms_norm(mixed_c, norm_mix_post[l])
            h_c = modulate(rms_norm(cx, norm_ffn_pre[l]), csh2, csc2)
            cx = cx + cgt2 * rms_norm(swiglu(h_c, ffn_w13[l], ffn_w2[l]), norm_ffn_post[l])
    return lat


import jax as _jax
import jax.numpy as _jnp

TWIN_FORMAT = 'train_step'
FWD_PARAMS = ['x', 'c', 'ctx', 'c_ctx', 'ada_w', 'ada_b', 'norm_mix_pre', 'norm_mix_post', 'norm_ffn_pre', 'norm_ffn_post', 'w_in', 'w_out', 'gla_gate_up', 'gla_gate_b', 'gla_norm', 'ssd_conv_w', 'ssd_conv_b', 'ssd_dt_bias', 'ssd_a_log', 'ssd_d', 'ssd_norm', 'ret_norm', 'ffn_w13', 'ffn_w2']
TWIN_WEIGHTS = ['c_ctx', 'ada_w', 'ada_b', 'norm_mix_pre', 'norm_mix_post', 'norm_ffn_pre', 'norm_ffn_post', 'w_in', 'w_out', 'gla_gate_up', 'gla_gate_b', 'gla_norm', 'ssd_conv_w', 'ssd_conv_b', 'ssd_dt_bias', 'ssd_a_log', 'ssd_d', 'ssd_norm', 'ret_norm', 'ffn_w13', 'ffn_w2']
TWIN_DIFF_INPUT = 'x'
TWIN_INPUTS = ['x', 'c', 'ctx', 'c_ctx', 'ada_w', 'ada_b', 'norm_mix_pre', 'norm_mix_post', 'norm_ffn_pre', 'norm_ffn_post', 'w_in', 'w_out', 'gla_gate_up', 'gla_gate_b', 'gla_norm', 'ssd_conv_w', 'ssd_conv_b', 'ssd_dt_bias', 'ssd_a_log', 'ssd_d', 'ssd_norm', 'ret_norm', 'ffn_w13', 'ffn_w2', 'loss_target', 'm_c_ctx', 'm_ada_w', 'm_ada_b', 'm_norm_mix_pre', 'm_norm_mix_post', 'm_norm_ffn_pre', 'm_norm_ffn_post', 'm_w_in', 'm_w_out', 'm_gla_gate_up', 'm_gla_gate_b', 'm_gla_norm', 'm_ssd_conv_w', 'm_ssd_conv_b', 'm_ssd_dt_bias', 'm_ssd_a_log', 'm_ssd_d', 'm_ssd_norm', 'm_ret_norm', 'm_ffn_w13', 'm_ffn_w2', 'v_c_ctx', 'v_ada_w', 'v_ada_b', 'v_norm_mix_pre', 'v_norm_mix_post', 'v_norm_ffn_pre', 'v_norm_ffn_post', 'v_w_in', 'v_w_out', 'v_gla_gate_up', 'v_gla_gate_b', 'v_gla_norm', 'v_ssd_conv_w', 'v_ssd_conv_b', 'v_ssd_dt_bias', 'v_ssd_a_log', 'v_ssd_d', 'v_ssd_norm', 'v_ret_norm', 'v_ffn_w13', 'v_ffn_w2']
TWIN_OUTPUTS = ['loss', 'grad_x', 'grad_c_ctx', 'grad_ada_w', 'grad_ada_b', 'grad_norm_mix_pre', 'grad_norm_mix_post', 'grad_norm_ffn_pre', 'grad_norm_ffn_post', 'grad_w_in', 'grad_w_out', 'grad_gla_gate_up', 'grad_gla_gate_b', 'grad_gla_norm', 'grad_ssd_conv_w', 'grad_ssd_conv_b', 'grad_ssd_dt_bias', 'grad_ssd_a_log', 'grad_ssd_d', 'grad_ssd_norm', 'grad_ret_norm', 'grad_ffn_w13', 'grad_ffn_w2', 'delta_c_ctx', 'delta_ada_w', 'delta_ada_b', 'delta_norm_mix_pre', 'delta_norm_mix_post', 'delta_norm_ffn_pre', 'delta_norm_ffn_post', 'delta_w_in', 'delta_w_out', 'delta_gla_gate_up', 'delta_gla_gate_b', 'delta_gla_norm', 'delta_ssd_conv_w', 'delta_ssd_conv_b', 'delta_ssd_dt_bias', 'delta_ssd_a_log', 'delta_ssd_d', 'delta_ssd_norm', 'delta_ret_norm', 'delta_ffn_w13', 'delta_ffn_w2', 'new_m_c_ctx', 'new_m_ada_w', 'new_m_ada_b', 'new_m_norm_mix_pre', 'new_m_norm_mix_post', 'new_m_norm_ffn_pre', 'new_m_norm_ffn_post', 'new_m_w_in', 'new_m_w_out', 'new_m_gla_gate_up', 'new_m_gla_gate_b', 'new_m_gla_norm', 'new_m_ssd_conv_w', 'new_m_ssd_conv_b', 'new_m_ssd_dt_bias', 'new_m_ssd_a_log', 'new_m_ssd_d', 'new_m_ssd_norm', 'new_m_ret_norm', 'new_m_ffn_w13', 'new_m_ffn_w2', 'new_v_c_ctx', 'new_v_ada_w', 'new_v_ada_b', 'new_v_norm_mix_pre', 'new_v_norm_mix_post', 'new_v_norm_ffn_pre', 'new_v_norm_ffn_post', 'new_v_w_in', 'new_v_w_out', 'new_v_gla_gate_up', 'new_v_gla_gate_b', 'new_v_gla_norm', 'new_v_ssd_conv_w', 'new_v_ssd_conv_b', 'new_v_ssd_dt_bias', 'new_v_ssd_a_log', 'new_v_ssd_d', 'new_v_ssd_norm', 'new_v_ret_norm', 'new_v_ffn_w13', 'new_v_ffn_w2']
TWIN_LEAF_KINDS = {'loss': 'loss', 'grad_x': 'grad_x', 'grad_c_ctx': 'grad_w', 'grad_ada_w': 'grad_w', 'grad_ada_b': 'grad_w', 'grad_norm_mix_pre': 'grad_w', 'grad_norm_mix_post': 'grad_w', 'grad_norm_ffn_pre': 'grad_w', 'grad_norm_ffn_post': 'grad_w', 'grad_w_in': 'grad_w', 'grad_w_out': 'grad_w', 'grad_gla_gate_up': 'grad_w', 'grad_gla_gate_b': 'grad_w', 'grad_gla_norm': 'grad_w', 'grad_ssd_conv_w': 'grad_w', 'grad_ssd_conv_b': 'grad_w', 'grad_ssd_dt_bias': 'grad_w', 'grad_ssd_a_log': 'grad_w', 'grad_ssd_d': 'grad_w', 'grad_ssd_norm': 'grad_w', 'grad_ret_norm': 'grad_w', 'grad_ffn_w13': 'grad_w', 'grad_ffn_w2': 'grad_w', 'delta_c_ctx': 'delta_w', 'delta_ada_w': 'delta_w', 'delta_ada_b': 'delta_w', 'delta_norm_mix_pre': 'delta_w', 'delta_norm_mix_post': 'delta_w', 'delta_norm_ffn_pre': 'delta_w', 'delta_norm_ffn_post': 'delta_w', 'delta_w_in': 'delta_w', 'delta_w_out': 'delta_w', 'delta_gla_gate_up': 'delta_w', 'delta_gla_gate_b': 'delta_w', 'delta_gla_norm': 'delta_w', 'delta_ssd_conv_w': 'delta_w', 'delta_ssd_conv_b': 'delta_w', 'delta_ssd_dt_bias': 'delta_w', 'delta_ssd_a_log': 'delta_w', 'delta_ssd_d': 'delta_w', 'delta_ssd_norm': 'delta_w', 'delta_ret_norm': 'delta_w', 'delta_ffn_w13': 'delta_w', 'delta_ffn_w2': 'delta_w', 'new_m_c_ctx': 'new_m', 'new_m_ada_w': 'new_m', 'new_m_ada_b': 'new_m', 'new_m_norm_mix_pre': 'new_m', 'new_m_norm_mix_post': 'new_m', 'new_m_norm_ffn_pre': 'new_m', 'new_m_norm_ffn_post': 'new_m', 'new_m_w_in': 'new_m', 'new_m_w_out': 'new_m', 'new_m_gla_gate_up': 'new_m', 'new_m_gla_gate_b': 'new_m', 'new_m_gla_norm': 'new_m', 'new_m_ssd_conv_w': 'new_m', 'new_m_ssd_conv_b': 'new_m', 'new_m_ssd_dt_bias': 'new_m', 'new_m_ssd_a_log': 'new_m', 'new_m_ssd_d': 'new_m', 'new_m_ssd_norm': 'new_m', 'new_m_ret_norm': 'new_m', 'new_m_ffn_w13': 'new_m', 'new_m_ffn_w2': 'new_m', 'new_v_c_ctx': 'new_v', 'new_v_ada_w': 'new_v', 'new_v_ada_b': 'new_v', 'new_v_norm_mix_pre': 'new_v', 'new_v_norm_mix_post': 'new_v', 'new_v_norm_ffn_pre': 'new_v', 'new_v_norm_ffn_post': 'new_v', 'new_v_w_in': 'new_v', 'new_v_w_out': 'new_v', 'new_v_gla_gate_up': 'new_v', 'new_v_gla_gate_b': 'new_v', 'new_v_gla_norm': 'new_v', 'new_v_ssd_conv_w': 'new_v', 'new_v_ssd_conv_b': 'new_v', 'new_v_ssd_dt_bias': 'new_v', 'new_v_ssd_a_log': 'new_v', 'new_v_ssd_d': 'new_v', 'new_v_ssd_norm': 'new_v', 'new_v_ret_norm': 'new_v', 'new_v_ffn_w13': 'new_v', 'new_v_ffn_w2': 'new_v'}


def _forward(args):
    return _fwd_reference(*[args[k] for k in FWD_PARAMS])


def _output_shape():
    def fwd():
        inp = _fwd_setup_inputs(0)
        return _fwd_reference(*[inp[k] for k in FWD_PARAMS])
    out = _jax.eval_shape(fwd)
    return out.shape, out.dtype

N_MICROBATCH = 1
ADAM_LR = 0.001
ADAM_B1 = 0.9
ADAM_B2 = 0.999
ADAM_EPS = 1e-08
ADAM_WD = 0.01
ADAM_STEP = 10
PER_EXAMPLE_BATCH_AXIS = {'x': 0, 'c': 0, 'ctx': 0, 'loss_target': 0}
SHARED_INPUTS = []
_WEIGHT_DTYPES = {'c_ctx': _jnp.float32, 'ada_w': _jnp.float32, 'ada_b': _jnp.float32, 'norm_mix_pre': _jnp.float32, 'norm_mix_post': _jnp.float32, 'norm_ffn_pre': _jnp.float32, 'norm_ffn_post': _jnp.float32, 'w_in': _jnp.float32, 'w_out': _jnp.float32, 'gla_gate_up': _jnp.float32, 'gla_gate_b': _jnp.float32, 'gla_norm': _jnp.float32, 'ssd_conv_w': _jnp.float32, 'ssd_conv_b': _jnp.float32, 'ssd_dt_bias': _jnp.float32, 'ssd_a_log': _jnp.float32, 'ssd_d': _jnp.float32, 'ssd_norm': _jnp.float32, 'ret_norm': _jnp.float32, 'ffn_w13': _jnp.float32, 'ffn_w2': _jnp.float32}
MOMENT_SCALE = {'c_ctx': 4.519536e-01, 'ada_w': 2.811844e+00, 'ada_b': 5.771958e+00, 'norm_mix_pre': 1.056718e+00, 'norm_mix_post': 1.327128e+01, 'norm_ffn_pre': 8.933357e-01, 'norm_ffn_post': 1.323488e+01, 'w_in': 1.149474e+00, 'w_out': 1.759459e+00, 'gla_gate_up': 2.700922e-01, 'gla_gate_b': 4.226747e-01, 'gla_norm': 9.706149e-01, 'ssd_conv_w': 1.313373e+00, 'ssd_conv_b': 1.825997e+00, 'ssd_dt_bias': 1.143118e+00, 'ssd_a_log': 3.513979e+00, 'ssd_d': 2.493187e+00, 'ssd_norm': 2.478405e+00, 'ret_norm': 9.236361e-01, 'ffn_w13': 7.090795e-01, 'ffn_w2': 1.326374e+00}


def _to_microbatches(a, axis):
    t = _jnp.moveaxis(a, axis, 0)
    t = t.reshape((N_MICROBATCH, t.shape[0] // N_MICROBATCH) + t.shape[1:])
    return _jnp.moveaxis(t, 1, axis + 1)


def setup_inputs(seed: int = 0) -> dict:
    inp = _fwd_setup_inputs(seed)
    key = _jax.random.fold_in(_jax.random.key(seed), 7919)
    shape, _ = _output_shape()
    out = dict(inp)
    out["loss_target"] = _jax.random.normal(_jax.random.fold_in(key, 0), shape, _jnp.float32)
    for i, name in enumerate(TWIN_WEIGHTS):
        w = inp[name].astype(_jnp.float32)
        if MOMENT_SCALE is None:
            s = _jnp.sqrt(_jnp.mean(_jnp.square(w)) + 1e-30)
        else:
            s = MOMENT_SCALE[name]
        km, kv = _jax.random.split(_jax.random.fold_in(key, i + 1))
        out[name] = w
        out["m_" + name] = s * _jax.random.normal(km, w.shape, _jnp.float32)
        out["v_" + name] = (s * s) * _jax.random.uniform(kv, w.shape, _jnp.float32, 0.5, 1.5)
    if N_MICROBATCH > 1:
        for name, axis in PER_EXAMPLE_BATCH_AXIS.items():
            out[name] = _to_microbatches(out[name], axis)
    return {'x': out['x'], 'c': out['c'], 'ctx': out['ctx'], 'c_ctx': out['c_ctx'], 'ada_w': out['ada_w'], 'ada_b': out['ada_b'], 'norm_mix_pre': out['norm_mix_pre'], 'norm_mix_post': out['norm_mix_post'], 'norm_ffn_pre': out['norm_ffn_pre'], 'norm_ffn_post': out['norm_ffn_post'], 'w_in': out['w_in'], 'w_out': out['w_out'], 'gla_gate_up': out['gla_gate_up'], 'gla_gate_b': out['gla_gate_b'], 'gla_norm': out['gla_norm'], 'ssd_conv_w': out['ssd_conv_w'], 'ssd_conv_b': out['ssd_conv_b'], 'ssd_dt_bias': out['ssd_dt_bias'], 'ssd_a_log': out['ssd_a_log'], 'ssd_d': out['ssd_d'], 'ssd_norm': out['ssd_norm'], 'ret_norm': out['ret_norm'], 'ffn_w13': out['ffn_w13'], 'ffn_w2': out['ffn_w2'], 'loss_target': out['loss_target'], 'm_c_ctx': out['m_c_ctx'], 'm_ada_w': out['m_ada_w'], 'm_ada_b': out['m_ada_b'], 'm_norm_mix_pre': out['m_norm_mix_pre'], 'm_norm_mix_post': out['m_norm_mix_post'], 'm_norm_ffn_pre': out['m_norm_ffn_pre'], 'm_norm_ffn_post': out['m_norm_ffn_post'], 'm_w_in': out['m_w_in'], 'm_w_out': out['m_w_out'], 'm_gla_gate_up': out['m_gla_gate_up'], 'm_gla_gate_b': out['m_gla_gate_b'], 'm_gla_norm': out['m_gla_norm'], 'm_ssd_conv_w': out['m_ssd_conv_w'], 'm_ssd_conv_b': out['m_ssd_conv_b'], 'm_ssd_dt_bias': out['m_ssd_dt_bias'], 'm_ssd_a_log': out['m_ssd_a_log'], 'm_ssd_d': out['m_ssd_d'], 'm_ssd_norm': out['m_ssd_norm'], 'm_ret_norm': out['m_ret_norm'], 'm_ffn_w13': out['m_ffn_w13'], 'm_ffn_w2': out['m_ffn_w2'], 'v_c_ctx': out['v_c_ctx'], 'v_ada_w': out['v_ada_w'], 'v_ada_b': out['v_ada_b'], 'v_norm_mix_pre': out['v_norm_mix_pre'], 'v_norm_mix_post': out['v_norm_mix_post'], 'v_norm_ffn_pre': out['v_norm_ffn_pre'], 'v_norm_ffn_post': out['v_norm_ffn_post'], 'v_w_in': out['v_w_in'], 'v_w_out': out['v_w_out'], 'v_gla_gate_up': out['v_gla_gate_up'], 'v_gla_gate_b': out['v_gla_gate_b'], 'v_gla_norm': out['v_gla_norm'], 'v_ssd_conv_w': out['v_ssd_conv_w'], 'v_ssd_conv_b': out['v_ssd_conv_b'], 'v_ssd_dt_bias': out['v_ssd_dt_bias'], 'v_ssd_a_log': out['v_ssd_a_log'], 'v_ssd_d': out['v_ssd_d'], 'v_ssd_norm': out['v_ssd_norm'], 'v_ret_norm': out['v_ret_norm'], 'v_ffn_w13': out['v_ffn_w13'], 'v_ffn_w2': out['v_ffn_w2']}


def _loss(weights, diff, rest, loss_target):
    with _jax.named_scope("forward"):
        args = {**rest, TWIN_DIFF_INPUT: diff, **{k: w.astype(_WEIGHT_DTYPES[k]) for k, w in weights.items()}}
        y = _forward(args)
    with _jax.named_scope("loss_head"):
        err = _jnp.square(y.astype(_jnp.float32) - loss_target)
        return 0.5 * _jnp.sum(_jnp.mean(err, axis=-1)) if err.ndim else 0.5 * err


def _adamw(w, g, m, v):
    m = ADAM_B1 * m + (1.0 - ADAM_B1) * g
    v = ADAM_B2 * v + (1.0 - ADAM_B2) * _jnp.square(g)
    m_hat = m / (1.0 - ADAM_B1 ** ADAM_STEP)
    v_hat = v / (1.0 - ADAM_B2 ** ADAM_STEP)
    delta = -ADAM_LR * (m_hat / (_jnp.sqrt(v_hat) + ADAM_EPS) + ADAM_WD * w)
    return delta, m, v


def reference(x, c, ctx, c_ctx, ada_w, ada_b, norm_mix_pre, norm_mix_post, norm_ffn_pre, norm_ffn_post, w_in, w_out, gla_gate_up, gla_gate_b, gla_norm, ssd_conv_w, ssd_conv_b, ssd_dt_bias, ssd_a_log, ssd_d, ssd_norm, ret_norm, ffn_w13, ffn_w2, loss_target, m_c_ctx, m_ada_w, m_ada_b, m_norm_mix_pre, m_norm_mix_post, m_norm_ffn_pre, m_norm_ffn_post, m_w_in, m_w_out, m_gla_gate_up, m_gla_gate_b, m_gla_norm, m_ssd_conv_w, m_ssd_conv_b, m_ssd_dt_bias, m_ssd_a_log, m_ssd_d, m_ssd_norm, m_ret_norm, m_ffn_w13, m_ffn_w2, v_c_ctx, v_ada_w, v_ada_b, v_norm_mix_pre, v_norm_mix_post, v_norm_ffn_pre, v_norm_ffn_post, v_w_in, v_w_out, v_gla_gate_up, v_gla_gate_b, v_gla_norm, v_ssd_conv_w, v_ssd_conv_b, v_ssd_dt_bias, v_ssd_a_log, v_ssd_d, v_ssd_norm, v_ret_norm, v_ffn_w13, v_ffn_w2):
    given = dict(x=x, c=c, ctx=ctx, c_ctx=c_ctx, ada_w=ada_w, ada_b=ada_b, norm_mix_pre=norm_mix_pre, norm_mix_post=norm_mix_post, norm_ffn_pre=norm_ffn_pre, norm_ffn_post=norm_ffn_post, w_in=w_in, w_out=w_out, gla_gate_up=gla_gate_up, gla_gate_b=gla_gate_b, gla_norm=gla_norm, ssd_conv_w=ssd_conv_w, ssd_conv_b=ssd_conv_b, ssd_dt_bias=ssd_dt_bias, ssd_a_log=ssd_a_log, ssd_d=ssd_d, ssd_norm=ssd_norm, ret_norm=ret_norm, ffn_w13=ffn_w13, ffn_w2=ffn_w2, loss_target=loss_target, m_c_ctx=m_c_ctx, m_ada_w=m_ada_w, m_ada_b=m_ada_b, m_norm_mix_pre=m_norm_mix_pre, m_norm_mix_post=m_norm_mix_post, m_norm_ffn_pre=m_norm_ffn_pre, m_norm_ffn_post=m_norm_ffn_post, m_w_in=m_w_in, m_w_out=m_w_out, m_gla_gate_up=m_gla_gate_up, m_gla_gate_b=m_gla_gate_b, m_gla_norm=m_gla_norm, m_ssd_conv_w=m_ssd_conv_w, m_ssd_conv_b=m_ssd_conv_b, m_ssd_dt_bias=m_ssd_dt_bias, m_ssd_a_log=m_ssd_a_log, m_ssd_d=m_ssd_d, m_ssd_norm=m_ssd_norm, m_ret_norm=m_ret_norm, m_ffn_w13=m_ffn_w13, m_ffn_w2=m_ffn_w2, v_c_ctx=v_c_ctx, v_ada_w=v_ada_w, v_ada_b=v_ada_b, v_norm_mix_pre=v_norm_mix_pre, v_norm_mix_post=v_norm_mix_post, v_norm_ffn_pre=v_norm_ffn_pre, v_norm_ffn_post=v_norm_ffn_post, v_w_in=v_w_in, v_w_out=v_w_out, v_gla_gate_up=v_gla_gate_up, v_gla_gate_b=v_gla_gate_b, v_gla_norm=v_gla_norm, v_ssd_conv_w=v_ssd_conv_w, v_ssd_conv_b=v_ssd_conv_b, v_ssd_dt_bias=v_ssd_dt_bias, v_ssd_a_log=v_ssd_a_log, v_ssd_d=v_ssd_d, v_ssd_norm=v_ssd_norm, v_ret_norm=v_ret_norm, v_ffn_w13=v_ffn_w13, v_ffn_w2=v_ffn_w2)
    weights = {n: given[n] for n in TWIN_WEIGHTS}
    shared = {n: given[n] for n in SHARED_INPUTS}
    per_example = {n: given[n] for n in ['x', 'c', 'ctx']}
    grad_fn = _jax.value_and_grad(_loss, argnums=(0, 1))

    def one_microbatch(ex, loss_target):
        ex = dict(ex)
        diff = ex.pop(TWIN_DIFF_INPUT)
        return grad_fn(weights, diff, {**shared, **ex}, loss_target)

    if N_MICROBATCH == 1:
        loss, (grad_w, grad_x) = one_microbatch(per_example, given["loss_target"])
    else:
        def body(carry, xs):
            loss_sum, grad_sum = carry
            l_k, (gw_k, gx_k) = one_microbatch(xs[0], xs[1])
            with _jax.named_scope("update"):
                return (loss_sum + l_k, _jax.tree.map(_jnp.add, grad_sum, gw_k)), gx_k

        init = (_jnp.zeros((), _jnp.float32), _jax.tree.map(_jnp.zeros_like, weights))
        (loss, grad_w), grad_x = _jax.lax.scan(body, init, (per_example, given["loss_target"]))
    with _jax.named_scope("update"):
        delta_w, new_m, new_v = {}, {}, {}
        for n in TWIN_WEIGHTS:
            delta_w[n], new_m[n], new_v[n] = _adamw(weights[n], grad_w[n], given["m_" + n], given["v_" + n])
    return (loss, grad_x, *[grad_w[n] for n in TWIN_WEIGHTS], *[delta_w[n] for n in TWIN_WEIGHTS],
            *[new_m[n] for n in TWIN_WEIGHTS], *[new_v[n] for n in TWIN_WEIGHTS])
```

```python
import functools

import numpy as np
import jax
import jax.numpy as jnp
from jax import lax
from jax.experimental import pallas as pl
from jax.experimental.pallas import tpu as pltpu

F32 = jnp.float32
BF16 = jnp.bfloat16
MESH = pl.DeviceIdType.MESH

D = 1024
DEPTH = 4
RMS_EPS = 1e-6
GLA_HEADS, GLA_DK, GLA_DV = 4, 32, 64
SSD_HEADS, SSD_DV, SSD_STATE, SSD_GROUPS = 8, 64, 128, 2
RET_HEADS, RET_DH = 4, 64
FFN_HIDDEN = 2816
IN_COLS = 3376
ADAM_LR, ADAM_B1, ADAM_B2, ADAM_EPS, ADAM_WD, ADAM_STEP = 0.001, 0.9, 0.999, 1e-08, 0.01, 10

P_XBC, P_RQ, P_RK, P_RV, P_RG, P_Z, P_GV, P_GR, P_GQ, P_GK, P_LR, P_DT = (
    0, 1024, 1280, 1536, 1792, 2048, 2560, 2816, 3072, 3200, 3328, 3456)
NP = 3584
_PERM = ((1312, 1024), (2352, 1024), (800, 512), (256, 256), (512, 256), (0, 128), (128, 128), (768, 32),
         (None, 96), (2336, 16), (None, 112))
_UNPERM = ((3072, 128), (3200, 128), (2560, 256), (2816, 256), (3328, 32), (2048, 512), (0, 1024), (3456, 16),
           (1024, 1024))

LANES = 128
VMEM_LIMIT = 56 * 1024 * 1024
TN_CHUNK = 512


def _cparams(sem=None):
    kw = dict(vmem_limit_bytes=VMEM_LIMIT)
    if sem is not None:
        kw["dimension_semantics"] = sem
    return pltpu.CompilerParams(**kw)


def _dot(a, b, ca, cb):
    return lax.dot_general(a.astype(BF16), b.astype(BF16), (((ca,), (cb,)), ((), ())),
                           preferred_element_type=F32)


@jax.custom_vjp
def mm_nn(a, b):
    return _dot(a, b, 1, 0)


@jax.custom_vjp
def mm_nt(a, b):
    return _dot(a, b, 1, 1)


@jax.custom_vjp
def mm_tn(a, b):
    return _dot(a, b, 0, 0)


mm_nn.defvjp(lambda a, b: (mm_nn(a, b), (a, b)), lambda r, g: (mm_nt(g, r[1]), mm_tn(r[0], g)))
mm_nt.defvjp(lambda a, b: (mm_nt(a, b), (a, b)), lambda r, g: (mm_nn(g, r[1]), mm_tn(g, r[0])))
mm_tn.defvjp(lambda a, b: (mm_tn(a, b), (a, b)), lambda r, g: (mm_nt(r[1], g), mm_nn(r[0], g)))


def _hi(a, b, ca, cb):
    return lax.dot_general(a, b, (((ca,), (cb,)), ((), ())), precision=lax.Precision.HIGHEST,
                           preferred_element_type=F32)


def dot_hi_plain(a, b):
    return _hi(a, b, 1, 0)


@jax.custom_vjp
def dot_hi(a, b):
    return _hi(a, b, 1, 0)


dot_hi.defvjp(lambda a, b: (dot_hi(a, b), (a, b)), lambda r, g: (_hi(g, r[1], 1, 1), _hi(r[0].T, g, 1, 0)))


def _sigmoid(x):
    return 1.0 / (1.0 + jnp.exp(-x))


def _silu(x):
    return x * _sigmoid(x)


def _softplus(x):
    return jnp.maximum(x, 0.0) + jnp.log(1.0 + jnp.exp(-jnp.abs(x)))


def _log_sigmoid(x):
    return -_softplus(-x)


def _order_mask(n, reverse):
    r = lax.broadcasted_iota(jnp.int32, (n, n), 0)
    c = lax.broadcasted_iota(jnp.int32, (n, n), 1)
    return ((c >= r) if reverse else (c <= r)).astype(F32)


def _chunk_scalar(qs, ks, vs, g, sts, *, mask, heads, rep, g_off):
    n = g.shape[0]
    cum = dot_hi(mask, g)
    tot = jnp.sum(g, axis=0, keepdims=True)
    lane = lax.broadcasted_iota(jnp.int32, (1, LANES), 1)
    outs, new = [], []
    for h in range(heads):
        pick = (lane == g_off + h).astype(F32)
        gh = jnp.sum(cum * pick, axis=1, keepdims=True)
        th = jnp.sum(tot * pick, axis=1, keepdims=True)
        gb = jnp.broadcast_to(gh, (n, n))
        dec = jnp.exp(jnp.minimum(gb - gb.T, 0.0)) * mask
        q, k, v = qs[h // rep], ks[h], vs[h]
        a = mm_nt(q, k) * dec
        outs.append(mm_nn(a, v) + mm_nt(q * jnp.exp(gh), sts[h]))
        new.append(sts[h] * jnp.exp(th) + mm_tn(v, k * jnp.exp(th - gh)))
    return outs, new


def _chunk_vector(q, k, vs, g, sts, *, mask, heads, dk):
    cum = dot_hi(mask, g)
    tot = jnp.sum(g, axis=0, keepdims=True)
    mid = 0.5 * tot
    qt = q * jnp.exp(jnp.minimum(cum - mid, 80.0))
    kt = k * jnp.exp(jnp.minimum(mid - cum, 80.0))
    qe = q * jnp.exp(cum)
    ke = k * jnp.exp(tot - cum)
    dec = jnp.exp(tot)
    lane = lax.broadcasted_iota(jnp.int32, (1, LANES), 1)
    outs, new = [], []
    for h in range(heads):
        hm = ((lane >= h * dk) & (lane < (h + 1) * dk)).astype(F32)
        a = mm_nt(qt * hm, kt) * mask
        outs.append(mm_nn(a, vs[h]) + mm_nt(qe * hm, sts[h]))
        new.append(sts[h] * dec + mm_tn(vs[h], ke * hm))
    return outs, new


class ScanCfg:
    def __init__(self, name, vector, heads, q_heads, dk, dv, chunk):
        self.name, self.vector, self.heads, self.q_heads = name, vector, heads, q_heads
        self.dk, self.dv, self.chunk = dk, dv, chunk
        self.rep = heads // q_heads
        self.st_k = LANES if vector else dk


def _scan_load(cfg, q_ref, k_ref, v_ref):
    f = lambda r, lo, w: r[:, lo:lo + w].astype(F32)
    vs = [f(v_ref, h * cfg.dv, cfg.dv) for h in range(cfg.heads)]
    if cfg.vector:
        return q_ref[...].astype(F32), k_ref[...].astype(F32), vs
    qs = [f(q_ref, i * cfg.dk, cfg.dk) for i in range(cfg.q_heads)]
    ks = [f(k_ref, h * cfg.dk, cfg.dk) for h in range(cfg.heads)]
    return qs, ks, vs


def _scan_fn(cfg, reverse, g_off):
    mask = _order_mask(cfg.chunk, reverse)
    if cfg.vector:
        return functools.partial(_chunk_vector, mask=mask, heads=cfg.heads, dk=cfg.dk)
    return functools.partial(_chunk_scalar, mask=mask, heads=cfg.heads, rep=cfg.rep, g_off=g_off)


def _chunk_of_step(s, n, nc, reverse):
    if not reverse:
        return s
    return jnp.where(s < nc, nc - 1 - s, n + nc - 1 - s)


def _arr_spec(a, c, pick):
    arr, w, cb = a
    return pl.BlockSpec((c, w), lambda s: (pick(s), cb))


def scan_fwd(cfg, q, k, v, g, *, t_ctx, reverse, g_off=0):
    t = q[0].shape[0]
    c, h = cfg.chunk, cfg.heads
    n, nc = t // c, t_ctx // c
    pick = lambda s: _chunk_of_step(s, n, nc, reverse)

    def body(q_ref, k_ref, v_ref, g_ref, o_ref, st_ref, state):
        @pl.when(pl.program_id(0) == 0)
        def _():
            state[...] = jnp.zeros_like(state)

        qq, kk, vs = _scan_load(cfg, q_ref, k_ref, v_ref)
        sts = [state[i] for i in range(h)]
        st_ref[...] = state[...]
        outs, new = _scan_fn(cfg, reverse, g_off)(qq, kk, vs, g_ref[...], sts)
        for i in range(h):
            o_ref[:, i * cfg.dv:(i + 1) * cfg.dv] = outs[i]
            state[i] = new[i]

    return pl.pallas_call(
        body, name=f"scan_fwd_{cfg.name}_{'b' if reverse else 'f'}",
        grid=(n,),
        in_specs=[_arr_spec(q, c, pick), _arr_spec(k, c, pick), _arr_spec(v, c, pick), _arr_spec(g, c, pick)],
        out_specs=[pl.BlockSpec((c, h * cfg.dv), lambda s: (pick(s), 0)),
                   pl.BlockSpec((None, h, cfg.dv, cfg.st_k), lambda s: (pick(s), 0, 0, 0))],
        out_shape=[jax.ShapeDtypeStruct((t, h * cfg.dv), F32),
                   jax.ShapeDtypeStruct((n, h, cfg.dv, cfg.st_k), F32)],
        scratch_shapes=[pltpu.VMEM((h, cfg.dv, cfg.st_k), F32)],
        compiler_params=_cparams(("arbitrary",)),
    )(q[0], k[0], v[0], g[0])


def scan_bwd(cfg, q, k, v, g, states, do, *, t_ctx, reverse, g_off=0):
    t = q[0].shape[0]
    c, h = cfg.chunk, cfg.heads
    n, nc = t // c, t_ctx // c
    pick = lambda s: _chunk_of_step(n - 1 - s, n, nc, reverse)
    qw = LANES if cfg.vector else cfg.q_heads * cfg.dk
    kw = LANES if cfg.vector else h * cfg.dk

    def body(q_ref, k_ref, v_ref, g_ref, st_ref, do_ref, dq_ref, dk_ref, dv_ref, dg_ref, dstate):
        @pl.when(pl.program_id(0) == 0)
        def _():
            dstate[...] = jnp.zeros_like(dstate)

        qq, kk, vs = _scan_load(cfg, q_ref, k_ref, v_ref)
        sts = [st_ref[i] for i in range(h)]
        dos = [do_ref[:, i * cfg.dv:(i + 1) * cfg.dv].astype(F32) for i in range(h)]
        _, vjp = jax.vjp(_scan_fn(cfg, reverse, g_off), qq, kk, vs, g_ref[...], sts)
        dqq, dkk, dvs, dg, dsts = vjp((dos, [dstate[i] for i in range(h)]))
        if cfg.vector:
            dq_ref[...] = dqq
            dk_ref[...] = dkk
        else:
            for i in range(cfg.q_heads):
                dq_ref[:, i * cfg.dk:(i + 1) * cfg.dk] = dqq[i]
            for i in range(h):
                dk_ref[:, i * cfg.dk:(i + 1) * cfg.dk] = dkk[i]
        dg_ref[...] = dg
        for i in range(h):
            dv_ref[:, i * cfg.dv:(i + 1) * cfg.dv] = dvs[i]
            dstate[i] = dsts[i]

    row = lambda w: pl.BlockSpec((c, w), lambda s: (pick(s), 0))
    return pl.pallas_call(
        body, name=f"scan_bwd_{cfg.name}_{'b' if reverse else 'f'}",
        grid=(n,),
        in_specs=[_arr_spec(q, c, pick), _arr_spec(k, c, pick), _arr_spec(v, c, pick), _arr_spec(g, c, pick),
                  pl.BlockSpec((None, h, cfg.dv, cfg.st_k), lambda s: (pick(s), 0, 0, 0)),
                  _arr_spec(do, c, pick)],
        out_specs=[row(qw), row(kw), row(h * cfg.dv), row(LANES)],
        out_shape=[jax.ShapeDtypeStruct((t, qw), F32), jax.ShapeDtypeStruct((t, kw), F32),
                   jax.ShapeDtypeStruct((t, h * cfg.dv), F32), jax.ShapeDtypeStruct((t, LANES), F32)],
        scratch_shapes=[pltpu.VMEM((h, cfg.dv, cfg.st_k), F32)],
        compiler_params=_cparams(("arbitrary",)),
    )(q[0], k[0], v[0], g[0], states, do[0])


def _rb(t_ctx):
    return min(256, t_ctx)


class _Rows:
    def __init__(self, t, t_ctx):
        self.rb = _rb(t_ctx)
        self.nb, self.ncb = t // self.rb, t_ctx // self.rb

    def seg(self, i):
        return jnp.where(i >= self.ncb, 1, 0)

    def rows(self, w, cb=0):
        return pl.BlockSpec((self.rb, w), lambda i: (i, cb))

    def arr(self, a):
        return self.rows(a[1], a[2])

    def const(self, *shape):
        return pl.BlockSpec(shape, lambda i: (0,) * len(shape))

    def per_seg(self, *shape):
        return pl.BlockSpec((None,) + shape, lambda i: (self.seg(i),) + (0,) * len(shape))

    def first(self):
        return pl.program_id(0) == 0

    def seg_first(self):
        i = pl.program_id(0)
        return (i == 0) | (i == self.ncb)


def _acc(ref, val, first):
    @pl.when(first)
    def _():
        ref[...] = val

    @pl.when(jnp.logical_not(first))
    def _():
        ref[...] += val


def _full(a):
    return (a, a.shape[1], 0)


def _norm_mod(x, nw, sh, sc):
    y = x * lax.rsqrt(jnp.mean(x * x, axis=-1, keepdims=True) + RMS_EPS)
    return (y * nw) * (1.0 + sc) + sh


def _gated_norm(z, gt, pw):
    y = z * lax.rsqrt(jnp.mean(z * z, axis=-1, keepdims=True) + RMS_EPS)
    return gt * (y * pw)


def _swiglu(g, u):
    return _silu(g) * u


def nm_matmul(x, nw, mod, w, *, sh, sc, t_ctx, out_dtype, name):
    t, n = x.shape[0], w.shape[1]
    r = _Rows(t, t_ctx)

    def body(x_ref, nw_ref, mod_ref, w_ref, o_ref, h_ref):
        h = _norm_mod(x_ref[...], nw_ref[...], mod_ref[sh:sh + 1, :], mod_ref[sc:sc + 1, :]).astype(BF16)
        h_ref[...] = h
        for j in range(n // TN_CHUNK):
            sl = slice(j * TN_CHUNK, (j + 1) * TN_CHUNK)
            o_ref[:, sl] = _dot(h, w_ref[:, sl], 1, 0).astype(out_dtype)

    return pl.pallas_call(
        body, name=name, grid=(r.nb,),
        in_specs=[r.rows(D), r.const(1, D), r.per_seg(6, D), r.const(D, n)],
        out_specs=[r.rows(n), r.rows(D)],
        out_shape=[jax.ShapeDtypeStruct((t, n), out_dtype), jax.ShapeDtypeStruct((t, D), BF16)],
        compiler_params=_cparams(("arbitrary",)),
    )(x, nw, mod, w)


def nm_bwd(dout, w, x, nw, mod, dres, *, sh, sc, t_ctx, name):
    t, n = x.shape[0], w.shape[1]
    r = _Rows(t, t_ctx)

    def body(do_ref, w_ref, x_ref, nw_ref, mod_ref, dres_ref, dx_ref, dnw_ref, dsh_ref, dsc_ref):
        dh = jnp.zeros((r.rb, D), F32)
        for j in range(n // TN_CHUNK):
            sl = slice(j * TN_CHUNK, (j + 1) * TN_CHUNK)
            dh = dh + _dot(do_ref[:, sl], w_ref[:, sl], 1, 1)
        _, vjp = jax.vjp(_norm_mod, x_ref[...], nw_ref[...], mod_ref[sh:sh + 1, :], mod_ref[sc:sc + 1, :])
        dx, dnw, dsh, dsc = vjp(dh)
        dx_ref[...] = dx + dres_ref[...]
        _acc(dnw_ref, dnw, r.first())
        _acc(dsh_ref, dsh, r.seg_first())
        _acc(dsc_ref, dsc, r.seg_first())

    return pl.pallas_call(
        body, name=name, grid=(r.nb,),
        in_specs=[r.rows(n), r.const(D, n), r.rows(D), r.const(1, D), r.per_seg(6, D), r.rows(D)],
        out_specs=[r.rows(D), r.const(1, D), r.per_seg(1, D), r.per_seg(1, D)],
        out_shape=[jax.ShapeDtypeStruct((t, D), F32), jax.ShapeDtypeStruct((1, D), F32),
                   jax.ShapeDtypeStruct((2, 1, D), F32), jax.ShapeDtypeStruct((2, 1, D), F32)],
        compiler_params=_cparams(("arbitrary",)),
    )(dout, w, x, nw, mod, dres)


def mm_postnorm(a, w, xres, mod, pw, *, gt, swiglu, t_ctx, name):
    t, k = xres.shape[0], w.shape[0]
    r = _Rows(t, t_ctx)

    def body(a_ref, w_ref, x_ref, mod_ref, pw_ref, xn_ref, z_ref, *act_ref):
        if swiglu:
            act = _swiglu(a_ref[:, :k].astype(F32), a_ref[:, k:].astype(F32)).astype(BF16)
            act_ref[0][...] = act
        else:
            act = a_ref[...]
        z = _dot(act, w_ref[...], 1, 0)
        z_ref[...] = z
        xn_ref[...] = x_ref[...] + _gated_norm(z, mod_ref[gt:gt + 1, :], pw_ref[...])

    outs = [jax.ShapeDtypeStruct((t, D), F32), jax.ShapeDtypeStruct((t, D), F32)]
    ospecs = [r.rows(D), r.rows(D)]
    if swiglu:
        outs.append(jax.ShapeDtypeStruct((t, k), BF16))
        ospecs.append(r.rows(k))
    return pl.pallas_call(
        body, name=name, grid=(r.nb,),
        in_specs=[r.rows(a.shape[1]), r.const(k, D), r.rows(D), r.per_seg(6, D), r.const(1, D)],
        out_specs=ospecs, out_shape=outs,
        compiler_params=_cparams(("arbitrary",)),
    )(a, w, xres, mod, pw)


def postnorm_bwd(dxn, z, mod, pw, w, u13, *, gt, t_ctx, name):
    t, k = z.shape[0], w.shape[0]
    r = _Rows(t, t_ctx)
    swiglu = u13 is not None

    def body(dxn_ref, z_ref, mod_ref, pw_ref, w_ref, *rest):
        if swiglu:
            u_ref, da_ref, dz_ref, dgt_ref, dpw_ref = rest
        else:
            da_ref, dz_ref, dgt_ref, dpw_ref = rest
        _, vjp = jax.vjp(_gated_norm, z_ref[...], mod_ref[gt:gt + 1, :], pw_ref[...])
        dz, dgt, dpw = vjp(dxn_ref[...])
        dzb = dz.astype(BF16)
        dz_ref[...] = dzb
        da = _dot(dzb, w_ref[...], 1, 1)
        if swiglu:
            _, vjp2 = jax.vjp(_swiglu, u_ref[:, :k].astype(F32), u_ref[:, k:].astype(F32))
            dg, du = vjp2(da)
            da_ref[:, :k] = dg.astype(BF16)
            da_ref[:, k:] = du.astype(BF16)
        else:
            da_ref[...] = da
        _acc(dgt_ref, dgt, r.seg_first())
        _acc(dpw_ref, dpw, r.first())

    ins = [dxn, z, mod, pw, w] + ([u13] if swiglu else [])
    in_specs = [r.rows(D), r.rows(D), r.per_seg(6, D), r.const(1, D), r.const(k, D)] + ([r.rows(2 * k)] if swiglu else [])
    da_shape = jax.ShapeDtypeStruct((t, 2 * k), BF16) if swiglu else jax.ShapeDtypeStruct((t, k), F32)
    return pl.pallas_call(
        body, name=name, grid=(r.nb,), in_specs=in_specs,
        out_specs=[r.rows(2 * k if swiglu else k), r.rows(D), r.per_seg(1, D), r.const(1, D)],
        out_shape=[da_shape, jax.ShapeDtypeStruct((t, D), BF16), jax.ShapeDtypeStruct((2, 1, D), F32),
                   jax.ShapeDtypeStruct((1, D), F32)],
        compiler_params=_cparams(("arbitrary",)),
    )(*ins)


def grad_matmul(a, b, *, name):
    t, k = a.shape
    n = b.shape[1]
    tt = t // 4
    tn = TN_CHUNK if n % TN_CHUNK == 0 else n
    tk = k if k <= 1024 else k // 2

    def body(a_ref, b_ref, o_ref):
        @pl.when(pl.program_id(2) == 0)
        def _():
            o_ref[...] = jnp.zeros_like(o_ref)

        o_ref[...] += _dot(a_ref[...], b_ref[...], 0, 0)

    return pl.pallas_call(
        body, name=name, grid=(k // tk, n // tn, t // tt),
        in_specs=[pl.BlockSpec((tt, tk), lambda i, j, s: (s, i)), pl.BlockSpec((tt, tn), lambda i, j, s: (s, j))],
        out_specs=pl.BlockSpec((tk, tn), lambda i, j, s: (i, j)),
        out_shape=jax.ShapeDtypeStruct((k, n), F32),
        compiler_params=_cparams(("parallel", "parallel", "arbitrary")),
    )(a, b)


def _shifted(cur, prev, nxt, d, has_prev, has_next):
    n = cur.shape[0]
    row = lax.broadcasted_iota(jnp.int32, cur.shape, 0)
    if d == 0:
        return cur
    if d < 0:
        return jnp.where(row < -d, pltpu.roll(prev, -d, 0) * has_prev, pltpu.roll(cur, -d, 0))
    return jnp.where(row >= n - d, pltpu.roll(nxt, n - d, 0) * has_next, pltpu.roll(cur, n - d, 0))


def _halo_specs(r, w, cb):
    return [pl.BlockSpec((r.rb, w), lambda i: (jnp.maximum(i - 1, 0), cb)),
            pl.BlockSpec((r.rb, w), lambda i: (i, cb)),
            pl.BlockSpec((r.rb, w), lambda i: (jnp.minimum(i + 1, r.nb - 1), cb))]


def _halo_flags(r):
    i = pl.program_id(0)
    has_prev = ((i != 0) & (i != r.ncb)).astype(F32)
    has_next = ((i != r.ncb - 1) & (i != r.nb - 1)).astype(F32)
    return has_prev, has_next


def conv_fwd(p, w8, b, *, t_ctx, name):
    t = p.shape[0]
    r = _Rows(t, t_ctx)
    cw = 1024

    def body(pp_ref, pc_ref, pn_ref, w_ref, b_ref, u_ref):
        hp, hn = _halo_flags(r)
        prev, cur, nxt = pp_ref[...], pc_ref[...], pn_ref[...]
        acc = jnp.broadcast_to(b_ref[...], cur.shape)
        for kk in range(5):
            acc = acc + w_ref[kk:kk + 1, :] * _shifted(cur, prev, nxt, kk - 2, hp, hn)
        u_ref[...] = acc

    return pl.pallas_call(
        body, name=name, grid=(r.nb,),
        in_specs=_halo_specs(r, cw, P_XBC // cw) + [r.const(8, cw), r.const(1, cw)],
        out_specs=r.rows(cw), out_shape=jax.ShapeDtypeStruct((t, cw), F32),
        compiler_params=_cparams(("arbitrary",)),
    )(p, p, p, w8, b)


def conv_bwd(p, du, w8, *, t_ctx, name):
    t = p.shape[0]
    r = _Rows(t, t_ctx)
    cw = 1024

    def body(pp_ref, pc_ref, pn_ref, dp_ref, dc_ref, dn_ref, w_ref, dx_ref, dw_ref, db_ref):
        hp, hn = _halo_flags(r)
        prev, cur, nxt = pp_ref[...], pc_ref[...], pn_ref[...]
        dprev, dcur, dnxt = dp_ref[...], dc_ref[...], dn_ref[...]
        acc = jnp.zeros_like(dcur)
        for kk in range(5):
            acc = acc + w_ref[kk:kk + 1, :] * _shifted(dcur, dprev, dnxt, 2 - kk, hp, hn)
        dx_ref[...] = acc
        first = r.first()
        for kk in range(5):
            _acc(dw_ref.at[kk:kk + 1, :], jnp.sum(_shifted(cur, prev, nxt, kk - 2, hp, hn) * dcur, axis=0, keepdims=True),
                 first)
        _acc(dw_ref.at[5:8, :], jnp.zeros((3, cw), F32), first)
        _acc(db_ref, jnp.sum(dcur, axis=0, keepdims=True), first)

    return pl.pallas_call(
        body, name=name, grid=(r.nb,),
        in_specs=_halo_specs(r, cw, P_XBC // cw) + _halo_specs(r, cw, 0) + [r.const(8, cw)],
        out_specs=[r.rows(cw), r.const(8, cw), r.const(1, cw)],
        out_shape=[jax.ShapeDtypeStruct((t, cw), F32), jax.ShapeDtypeStruct((8, cw), F32),
                   jax.ShapeDtypeStruct((1, cw), F32)],
        compiler_params=_cparams(("arbitrary",)),
    )(p, p, p, du, du, du, w8)


def _ssd_act(u_x, u_b0, u_b1, u_c0, u_c1, dtraw, dtb, alog):
    dt = _softplus(dtraw + dtb)
    g = dt * (-jnp.exp(alog))
    bs = [_silu(u_b0), _silu(u_b1)]
    lane = lax.broadcasted_iota(jnp.int32, (1, LANES), 1)
    kf, kb = [], []
    for h in range(SSD_HEADS):
        for lst, off in ((kf, 0), (kb, SSD_HEADS)):
            col = jnp.sum(dt * (lane == off + h).astype(F32), axis=1, keepdims=True)
            lst.append(bs[h // (SSD_HEADS // SSD_GROUPS)] * col)
    return _silu(u_x), bs[0], bs[1], _silu(u_c0), _silu(u_c1), kf, kb, g


def _ssd_act_inputs(u_ref, dt_ref, dtb_ref, alog_ref):
    return (u_ref[:, 0:512], u_ref[:, 512:640], u_ref[:, 640:768], u_ref[:, 768:896], u_ref[:, 896:1024],
            dt_ref[...], dtb_ref[...], alog_ref[...])


def ssd_act_fwd(u, p, dtb, alog, *, t_ctx, name):
    t = u.shape[0]
    r = _Rows(t, t_ctx)

    def body(u_ref, dt_ref, dtb_ref, alog_ref, act_ref, kf_ref, kb_ref, g_ref):
        xs, b0, b1, c0, c1, kf, kb, g = _ssd_act(*_ssd_act_inputs(u_ref, dt_ref, dtb_ref, alog_ref))
        act_ref[:, 0:512] = xs
        for off, val in ((512, b0), (640, b1), (768, c0), (896, c1)):
            act_ref[:, off:off + 128] = val
        for h in range(SSD_HEADS):
            kf_ref[:, h * 128:(h + 1) * 128] = kf[h].astype(BF16)
            kb_ref[:, h * 128:(h + 1) * 128] = kb[h].astype(BF16)
        g_ref[...] = g

    return pl.pallas_call(
        body, name=name, grid=(r.nb,),
        in_specs=[r.rows(1024), r.rows(LANES, P_DT // LANES), r.const(1, LANES), r.const(1, LANES)],
        out_specs=[r.rows(1024), r.rows(1024), r.rows(1024), r.rows(LANES)],
        out_shape=[jax.ShapeDtypeStruct((t, 1024), F32), jax.ShapeDtypeStruct((t, 1024), BF16),
                   jax.ShapeDtypeStruct((t, 1024), BF16), jax.ShapeDtypeStruct((t, LANES), F32)],
        compiler_params=_cparams(("arbitrary",)),
    )(u, p, dtb, alog)


def ssd_act_bwd(u, p, dtb, alog, dxs_list, dc_list, dkf, dkb, dg_list, *, t_ctx, name):
    t = u.shape[0]
    r = _Rows(t, t_ctx)
    nx, ncm, ng = len(dxs_list), len(dc_list), len(dg_list)

    def body(*refs):
        u_ref, dt_ref, dtb_ref, alog_ref = refs[:4]
        rest = list(refs[4:])
        dxs = sum(rf[...] for rf in rest[:nx])
        dcm = sum(rf[...] for rf in rest[nx:nx + ncm])
        dkf_ref, dkb_ref = rest[nx + ncm:nx + ncm + 2]
        dgs = sum(rf[...] for rf in rest[nx + ncm + 2:nx + ncm + 2 + ng])
        du_ref, ddt_ref, ddtb_ref, dalog_ref = rest[nx + ncm + 2 + ng:]
        _, vjp = jax.vjp(_ssd_act, *_ssd_act_inputs(u_ref, dt_ref, dtb_ref, alog_ref))
        zero = jnp.zeros((r.rb, 128), F32)
        ct = (dxs, zero, zero, dcm[:, 0:128], dcm[:, 128:256],
              [dkf_ref[:, h * 128:(h + 1) * 128] for h in range(SSD_HEADS)],
              [dkb_ref[:, h * 128:(h + 1) * 128] for h in range(SSD_HEADS)], dgs)
        dux, db0, db1, dc0, dc1, ddt, ddtb, dalog = vjp(ct)
        du_ref[:, 0:512] = dux
        for off, val in ((512, db0), (640, db1), (768, dc0), (896, dc1)):
            du_ref[:, off:off + 128] = val
        ddt_ref[...] = ddt
        _acc(ddtb_ref, ddtb, r.first())
        _acc(dalog_ref, dalog, r.first())

    ins = [u, p, dtb, alog] + list(dxs_list) + list(dc_list) + [dkf, dkb] + list(dg_list)
    in_specs = ([r.rows(1024), r.rows(LANES, P_DT // LANES), r.const(1, LANES), r.const(1, LANES)]
                + [r.rows(512)] * nx + [r.rows(256)] * ncm + [r.rows(1024)] * 2 + [r.rows(LANES)] * ng)
    return pl.pallas_call(
        body, name=name, grid=(r.nb,), in_specs=in_specs,
        out_specs=[r.rows(1024), r.rows(LANES), r.const(1, LANES), r.const(1, LANES)],
        out_shape=[jax.ShapeDtypeStruct((t, 1024), F32), jax.ShapeDtypeStruct((t, LANES), F32),
                   jax.ShapeDtypeStruct((1, LANES), F32), jax.ShapeDtypeStruct((1, LANES), F32)],
        compiler_params=_cparams(("arbitrary",)),
    )(*ins)


GLA_RANK = 16


def _gla_pre(q, lr, ups_f, ups_b, bf, bb):
    lane = lax.broadcasted_iota(jnp.int32, (1, LANES), 1)
    zf, zb = bf, bb
    for i in range(GLA_RANK):
        zf = zf + jnp.sum(lr * (lane == i).astype(F32), axis=1, keepdims=True) * ups_f[i]
        zb = zb + jnp.sum(lr * (lane == GLA_RANK + i).astype(F32), axis=1, keepdims=True) * ups_b[i]
    return q * (GLA_DK ** -0.5), _log_sigmoid(zf) * (1.0 / 16.0), _log_sigmoid(zb) * (1.0 / 16.0)


def _gla_pre_inputs(q_ref, lr_ref, up_ref, b_ref):
    rows = lambda d: [up_ref[d, i:i + 1, :] for i in range(GLA_RANK)]
    return q_ref[...], lr_ref[...], rows(0), rows(1), b_ref[0], b_ref[1]


def _gla_pre_specs(r):
    return [r.rows(LANES, P_GQ // LANES), r.rows(LANES, P_LR // LANES), r.const(2, GLA_RANK, LANES),
            r.const(2, 1, LANES)]


def gla_pre_fwd(p, up, gbias, *, t_ctx, name):
    t = p.shape[0]
    r = _Rows(t, t_ctx)

    def body(q_ref, lr_ref, up_ref, b_ref, qs_ref, gf_ref, gb_ref):
        qs, gf, gb = _gla_pre(*_gla_pre_inputs(q_ref, lr_ref, up_ref, b_ref))
        qs_ref[...] = qs
        gf_ref[...] = gf
        gb_ref[...] = gb

    sd = jax.ShapeDtypeStruct((t, LANES), F32)
    return pl.pallas_call(
        body, name=name, grid=(r.nb,), in_specs=_gla_pre_specs(r),
        out_specs=[r.rows(LANES)] * 3, out_shape=[sd, sd, sd],
        compiler_params=_cparams(("arbitrary",)),
    )(p, p, up, gbias)


def gla_pre_bwd(p, up, gbias, dq_list, dgf, dgb, *, t_ctx, name):
    t = p.shape[0]
    r = _Rows(t, t_ctx)
    nq = len(dq_list)

    def body(*refs):
        q_ref, lr_ref, up_ref, b_ref = refs[:4]
        dq = sum(rf[...] for rf in refs[4:4 + nq])
        dgf_ref, dgb_ref, dqo_ref, dlr_ref, dup_ref, db_ref = refs[4 + nq:]
        _, vjp = jax.vjp(_gla_pre, *_gla_pre_inputs(q_ref, lr_ref, up_ref, b_ref))
        dqo, dlr, dups_f, dups_b, dbf, dbb = vjp((dq, dgf_ref[...], dgb_ref[...]))
        dqo_ref[...] = dqo
        dlr_ref[...] = dlr

        def write(add):
            for d, (dups, dbias) in enumerate(((dups_f, dbf), (dups_b, dbb))):
                for i in range(GLA_RANK):
                    dup_ref[d, i:i + 1, :] = dups[i] + (dup_ref[d, i:i + 1, :] if add else 0.0)
                db_ref[d] = dbias + (db_ref[d] if add else 0.0)

        first = r.first()
        pl.when(first)(lambda: write(False))
        pl.when(jnp.logical_not(first))(lambda: write(True))

    sd = jax.ShapeDtypeStruct((t, LANES), F32)
    return pl.pallas_call(
        body, name=name, grid=(r.nb,), in_specs=_gla_pre_specs(r) + [r.rows(LANES)] * (nq + 2),
        out_specs=[r.rows(LANES), r.rows(LANES), r.const(2, GLA_RANK, LANES), r.const(2, 1, LANES)],
        out_shape=[sd, sd, jax.ShapeDtypeStruct((2, GLA_RANK, LANES), F32), jax.ShapeDtypeStruct((2, 1, LANES), F32)],
        compiler_params=_cparams(("arbitrary",)),
    )(p, p, up, gbias, *dq_list, dgf, dgb)


def _swap_halves(v):
    lane = lax.broadcasted_iota(jnp.int32, v.shape, 1)
    w = v.shape[1]
    return jnp.where((lane & 63) >= 32, pltpu.roll(v, 32, 1), pltpu.roll(v, w - 32, 1))


def rope_pair(q_list, k_list, cos, sin, *, transpose, t_ctx, name):
    t = cos.shape[0]
    r = _Rows(t, t_ctx)
    nq, nk = len(q_list), len(k_list)
    w = RET_HEADS * RET_DH

    def body(*refs):
        q = sum(rf[...] for rf in refs[:nq])
        k = sum(rf[...] for rf in refs[nq:nq + nk])
        cos_ref, sin_ref, qo_ref, ko_ref = refs[nq + nk:]
        cs, sn = cos_ref[...], sin_ref[...]
        if transpose:
            rot = lambda v: v * cs + _swap_halves(v * sn)
        else:
            rot = lambda v: v * cs + _swap_halves(v) * sn
        qo_ref[...] = rot(q) * (RET_DH ** -0.5)
        ko_ref[...] = rot(k)

    sd = jax.ShapeDtypeStruct((t, w), F32)
    return pl.pallas_call(
        body, name=name, grid=(r.nb,),
        in_specs=[r.arr(a) for a in list(q_list) + list(k_list)] + [r.rows(w), r.rows(w)],
        out_specs=[r.rows(w), r.rows(w)], out_shape=[sd, sd],
        compiler_params=_cparams(("arbitrary",)),
    )(*[a[0] for a in list(q_list) + list(k_list)], cos, sin)


def _gla_post(o_f, o_b, r_gate, nw, pm):
    o = o_f + o_b
    ms = dot_hi(o * o, pm)
    return o * lax.rsqrt(ms + RMS_EPS) * nw * _silu(r_gate)


def _ssd_post(o_f, o_b, xs, z, dsk, nw):
    y = (o_f + o_b + dsk * xs) * _silu(z)
    return y * lax.rsqrt(jnp.mean(y * y, axis=-1, keepdims=True) + RMS_EPS) * nw


def _ret_post(o_f, o_b, gate, nw, pm):
    o = o_f + o_b
    xc = o - dot_hi(o, pm)
    var = dot_hi(xc * xc, pm)
    return xc * lax.rsqrt(var + RMS_EPS) * nw * _silu(gate)


def _mix_post_specs(r):
    return [r.rows(256), r.rows(256), r.rows(512), r.rows(512), r.rows(256), r.rows(256),
            r.rows(256, P_GR // 256), r.rows(512, P_Z // 512), r.rows(256, P_RG // 256),
            r.rows(512, 0),
            r.const(1, 256), r.const(1, 512), r.const(1, 512), r.const(1, 256), r.const(256, 256)]


def mix_post_fwd(go_f, go_b, so_f, so_b, ro_f, ro_b, p, act, gla_nw, dsk, ssd_nw, ret_nw, pm, *, t_ctx, name):
    t = p.shape[0]
    r = _Rows(t, t_ctx)

    def body(gf, gb, sf, sb, rf, rbk, rg, z, gg, xs, gnw, dk, snw, rnw, pmr, y_ref):
        y_ref[:, 0:256] = _gla_post(gf[...], gb[...], rg[...], gnw[...], pmr[...]).astype(BF16)
        y_ref[:, 256:768] = _ssd_post(sf[...], sb[...], xs[...], z[...], dk[...], snw[...]).astype(BF16)
        y_ref[:, 768:1024] = _ret_post(rf[...], rbk[...], gg[...], rnw[...], pmr[...]).astype(BF16)

    return pl.pallas_call(
        body, name=name, grid=(r.nb,), in_specs=_mix_post_specs(r),
        out_specs=r.rows(D), out_shape=jax.ShapeDtypeStruct((t, D), BF16),
        compiler_params=_cparams(("arbitrary",)),
    )(go_f, go_b, so_f, so_b, ro_f, ro_b, p, p, p, act, gla_nw, dsk, ssd_nw, ret_nw, pm)


def mix_post_bwd(go_f, go_b, so_f, so_b, ro_f, ro_b, p, act, gla_nw, dsk, ssd_nw, ret_nw, pm, dy, *, t_ctx, name):
    t = p.shape[0]
    r = _Rows(t, t_ctx)

    def body(gf, gb, sf, sb, rf, rbk, rg, z, gg, xs, gnw, dk, snw, rnw, pmr, dy_ref,
             dgo, dso, dro, drg, dz, dgg, dxs, dgnw, ddk, dsnw, drnw):
        first = r.first()
        pmv = pmr[...]
        _, vjp = jax.vjp(lambda a, b, c, d: _gla_post(a, b, c, d, pmv), gf[...], gb[...], rg[...], gnw[...])
        a, _, c, d = vjp(dy_ref[:, 0:256])
        dgo[...] = a
        drg[...] = c
        _acc(dgnw, d, first)
        _, vjp = jax.vjp(_ssd_post, sf[...], sb[...], xs[...], z[...], dk[...], snw[...])
        a, _, c, d, e, f = vjp(dy_ref[:, 256:768])
        dso[...] = a
        dxs[...] = c
        dz[...] = d
        _acc(ddk, e, first)
        _acc(dsnw, f, first)
        _, vjp = jax.vjp(lambda a, b, c, d: _ret_post(a, b, c, d, pmv), rf[...], rbk[...], gg[...], rnw[...])
        a, _, c, d = vjp(dy_ref[:, 768:1024])
        dro[...] = a
        dgg[...] = c
        _acc(drnw, d, first)

    sd = lambda w: jax.ShapeDtypeStruct((t, w), F32)
    sp = lambda w: jax.ShapeDtypeStruct((1, w), F32)
    return pl.pallas_call(
        body, name=name, grid=(r.nb,), in_specs=_mix_post_specs(r) + [r.rows(D)],
        out_specs=[r.rows(256), r.rows(512), r.rows(256), r.rows(256), r.rows(512), r.rows(256), r.rows(512),
                   r.const(1, 256), r.const(1, 512), r.const(1, 512), r.const(1, 256)],
        out_shape=[sd(256), sd(512), sd(256), sd(256), sd(512), sd(256), sd(512), sp(256), sp(512), sp(512), sp(256)],
        compiler_params=_cparams(("arbitrary",)),
    )(go_f, go_b, so_f, so_b, ro_f, ro_b, p, p, p, act, gla_nw, dsk, ssd_nw, ret_nw, pm, dy)


def dp_assemble(pieces, *, t_ctx, name):
    t = pieces[0][2][0].shape[0]
    r = _Rows(t, t_ctx)
    flat = [a for _, _, arrs in pieces for a in arrs]

    def body(*refs):
        o_ref = refs[-1]
        i = 0
        for start, w, arrs in pieces:
            tot = sum(refs[i + j][...] for j in range(len(arrs)))
            i += len(arrs)
            o_ref[:, start:start + w] = tot.astype(BF16)

    return pl.pallas_call(
        body, name=name, grid=(r.nb,),
        in_specs=[r.rows(w) for _, w, arrs in pieces for _ in arrs],
        out_specs=r.rows(NP), out_shape=jax.ShapeDtypeStruct((t, NP), BF16),
        compiler_params=_cparams(("arbitrary",)),
    )(*flat)


def loss_head(xs, target, *, t_ctx, name):
    t = xs.shape[0]
    r = _Rows(t, t_ctx)

    def body(x_ref, t_ref, l_ref, dx_ref):
        i = pl.program_id(0)
        lat = (i >= r.ncb).astype(F32)
        diff = (x_ref[...] - t_ref[...]) * lat
        dx_ref[...] = diff * (1.0 / D)
        part = jnp.sum(jnp.sum(diff * diff, axis=1, keepdims=True), axis=0, keepdims=True) * (0.5 / D)
        _acc(l_ref, jnp.broadcast_to(part, (1, LANES)), r.first())

    return pl.pallas_call(
        body, name=name, grid=(r.nb,),
        in_specs=[r.rows(D), pl.BlockSpec((r.rb, D), lambda i: (jnp.maximum(i - r.ncb, 0), 0))],
        out_specs=[r.const(1, LANES), r.rows(D)],
        out_shape=[jax.ShapeDtypeStruct((1, LANES), F32), jax.ShapeDtypeStruct((t, D), F32)],
        compiler_params=_cparams(("arbitrary",)),
    )(xs, target)


def _ada(cg, w):
    return dot_hi_plain(_silu(cg), w)


def ada_fwd(cg, ada_w, *, name):
    nl, _, n = ada_w.shape

    def body(c_ref, w_ref, o_ref):
        o_ref[...] = _ada(c_ref[...], w_ref[...])

    return pl.pallas_call(
        body, name=name, grid=(nl,),
        in_specs=[pl.BlockSpec((16, D), lambda l: (0, 0)), pl.BlockSpec((None, D, n), lambda l: (l, 0, 0))],
        out_specs=pl.BlockSpec((None, 16, n), lambda l: (l, 0, 0)),
        out_shape=jax.ShapeDtypeStruct((nl, 16, n), F32),
        compiler_params=_cparams(("arbitrary",)),
    )(cg, ada_w)


def ada_bwd(cg, ada_w, dmod, *, name):
    nl, _, n = ada_w.shape

    def body(c_ref, w_ref, g_ref, dw_ref, dc_ref):
        _, vjp = jax.vjp(_ada, c_ref[...], w_ref[...])
        dc, dw = vjp(g_ref[...])
        dw_ref[...] = dw
        _acc(dc_ref, dc, pl.program_id(0) == 0)

    return pl.pallas_call(
        body, name=name, grid=(nl,),
        in_specs=[pl.BlockSpec((16, D), lambda l: (0, 0)), pl.BlockSpec((None, D, n), lambda l: (l, 0, 0)),
                  pl.BlockSpec((None, 16, n), lambda l: (l, 0, 0))],
        out_specs=[pl.BlockSpec((None, D, n), lambda l: (l, 0, 0)), pl.BlockSpec((16, D), lambda l: (0, 0))],
        out_shape=[jax.ShapeDtypeStruct((nl, D, n), F32), jax.ShapeDtypeStruct((16, D), F32)],
        compiler_params=_cparams(("arbitrary",)),
    )(cg, ada_w, dmod)


def _row_block(rows):
    for br in range(512, 7, -8):
        if rows % br == 0:
            return br
    return rows


def adamw(w, g, m, v, *, name):
    rows, cols = w.shape
    br = _row_block(rows)

    def body(w_ref, g_ref, m_ref, v_ref, d_ref, nm_ref, nv_ref):
        gg = g_ref[...]
        nm = ADAM_B1 * m_ref[...] + (1.0 - ADAM_B1) * gg
        nv = ADAM_B2 * v_ref[...] + (1.0 - ADAM_B2) * (gg * gg)
        m_hat = nm / (1.0 - ADAM_B1 ** ADAM_STEP)
        v_hat = nv / (1.0 - ADAM_B2 ** ADAM_STEP)
        d_ref[...] = -ADAM_LR * (m_hat / (jnp.sqrt(v_hat) + ADAM_EPS) + ADAM_WD * w_ref[...])
        nm_ref[...] = nm
        nv_ref[...] = nv

    spec = pl.BlockSpec((br, cols), lambda i: (i, 0))
    sd = jax.ShapeDtypeStruct((rows, cols), F32)
    return pl.pallas_call(
        body, name=name, grid=(rows // br,), in_specs=[spec] * 4, out_specs=[spec] * 3, out_shape=[sd] * 3,
        compiler_params=_cparams(("parallel",)),
    )(w, g, m, v)


def sum_leading(a, out_dtype, *, name):
    n, rows, cols = a.shape
    br = _row_block(rows)

    def body(a_ref, o_ref):
        acc = a_ref[0].astype(F32)
        for i in range(1, n):
            acc = acc + a_ref[i].astype(F32)
        o_ref[...] = acc.astype(out_dtype)

    return pl.pallas_call(
        body, name=name, grid=(rows // br,),
        in_specs=[pl.BlockSpec((n, br, cols), lambda i: (0, i, 0))],
        out_specs=pl.BlockSpec((br, cols), lambda i: (i, 0)),
        out_shape=jax.ShapeDtypeStruct((rows, cols), out_dtype),
        compiler_params=_cparams(("parallel",)),
    )(a)


def _position():
    return lax.axis_index("x"), lax.axis_index("y"), lax.axis_index("c")


def _other_chips(x, y):
    return [(1 - x, y), (x, 1 - y), (1 - x, 1 - y)]


def all_gather8(blk, *, name):
    m_per, n = blk.shape

    def body(x_ref, out_ref, send_sems, recv_sems, local_sem):
        x, y, c = _position()
        me, sibling = (x, y, c), (x, y, 1 - c)
        chips = _other_chips(x, y)

        def rows(px, py, pc):
            return out_ref.at[pl.ds((4 * px + 2 * py + pc) * m_per, m_per), :]

        def copy(k, block, to, src=None):
            return pltpu.make_async_remote_copy(
                src_ref=rows(*block) if src is None else src, dst_ref=rows(*block),
                send_sem=send_sems.at[k], recv_sem=recv_sems.at[k], device_id=to, device_id_type=MESH)

        mine = pltpu.make_async_copy(x_ref, rows(*me), local_sem)
        mine.start()
        first = [copy(0, me, sibling, src=x_ref)]
        first += [copy(1 + j, me, (*chip, c), src=x_ref) for j, chip in enumerate(chips)]
        for cp in first:
            cp.start()
        passed = [copy(4 + j, (*chip, c), sibling) for j, chip in enumerate(chips)]
        for j, chip in enumerate(chips):
            copy(1 + j, (*chip, c), me).wait_recv()
            passed[j].start()
        copy(0, sibling, me).wait_recv()
        for j, chip in enumerate(chips):
            copy(4 + j, (*chip, 1 - c), me).wait_recv()
        for cp in first + passed:
            cp.wait_send()
        mine.wait()

    return pl.pallas_call(
        body, name=name,
        out_shape=jax.ShapeDtypeStruct((8 * m_per, n), blk.dtype),
        in_specs=[pl.BlockSpec(memory_space=pltpu.VMEM)],
        out_specs=pl.BlockSpec(memory_space=pltpu.VMEM),
        scratch_shapes=[pltpu.SemaphoreType.DMA((7,)), pltpu.SemaphoreType.DMA((7,)), pltpu.SemaphoreType.DMA],
        compiler_params=pltpu.CompilerParams(vmem_limit_bytes=VMEM_LIMIT),
    )(blk)


_ANY = pl.BlockSpec(memory_space=pl.ANY)


def pair_exchange(arrs, *, name):
    n = len(arrs)

    def body(*refs):
        ins, outs, send_sems, recv_sems = refs[:n], refs[n:2 * n], refs[2 * n], refs[2 * n + 1]
        x, y, c = _position()
        cps = [pltpu.make_async_remote_copy(src_ref=ins[k], dst_ref=outs[k], send_sem=send_sems.at[k],
                                            recv_sem=recv_sems.at[k], device_id=(x, y, 1 - c), device_id_type=MESH)
               for k in range(n)]
        for cp in cps:
            cp.start()
        for cp in cps:
            cp.wait()

    return pl.pallas_call(
        body, name=name, out_shape=[jax.ShapeDtypeStruct(a.shape, a.dtype) for a in arrs],
        in_specs=[_ANY] * n, out_specs=[_ANY] * n,
        scratch_shapes=[pltpu.SemaphoreType.DMA((n,)), pltpu.SemaphoreType.DMA((n,))],
    )(*arrs)


def chip_exchange(arrs, *, gather, name):
    n = len(arrs)

    def body(*refs):
        ins, outs = refs[:n], refs[n:2 * n]
        send_sems, recv_sems, local_sems = refs[2 * n:]
        x, y, c = _position()
        me = 2 * x + y
        chips = _other_chips(x, y)
        local = [pltpu.make_async_copy(ins[k] if gather else ins[k].at[me], outs[k].at[me], local_sems.at[k])
                 for k in range(n)]
        for cp in local:
            cp.start()
        cps = []
        for k in range(n):
            for j, (px, py) in enumerate(chips):
                peer = 2 * px + py
                cps.append(pltpu.make_async_remote_copy(
                    src_ref=ins[k] if gather else ins[k].at[peer], dst_ref=outs[k].at[me],
                    send_sem=send_sems.at[3 * k + j], recv_sem=recv_sems.at[3 * k + j],
                    device_id=(px, py, c), device_id_type=MESH))
        for cp in cps:
            cp.start()
        for k in range(n):
            for j, (px, py) in enumerate(chips):
                peer = 2 * px + py
                pltpu.make_async_remote_copy(
                    src_ref=ins[k] if gather else ins[k].at[peer], dst_ref=outs[k].at[peer],
                    send_sem=send_sems.at[3 * k + j], recv_sem=recv_sems.at[3 * k + j],
                    device_id=(px, py, c), device_id_type=MESH).wait_recv()
        for cp in cps:
            cp.wait_send()
        for cp in local:
            cp.wait()

    out_shape = [jax.ShapeDtypeStruct(((4,) + a.shape) if gather else a.shape, a.dtype) for a in arrs]
    return pl.pallas_call(
        body, name=name, out_shape=out_shape, in_specs=[_ANY] * n, out_specs=[_ANY] * n,
        scratch_shapes=[pltpu.SemaphoreType.DMA((3 * n,)), pltpu.SemaphoreType.DMA((3 * n,)),
                        pltpu.SemaphoreType.DMA((n,))],
    )(*arrs)


def sum_arrays(arrs, out_dtype, *, name):
    rows, cols = arrs[0].shape
    br = _row_block(rows)

    def body(*refs):
        acc = refs[0][...].astype(F32)
        for rf in refs[1:-1]:
            acc = acc + rf[...].astype(F32)
        refs[-1][...] = acc.astype(out_dtype)

    spec = pl.BlockSpec((br, cols), lambda i: (i, 0))
    return pl.pallas_call(
        body, name=name, grid=(rows // br,), in_specs=[spec] * len(arrs), out_specs=spec,
        out_shape=jax.ShapeDtypeStruct((rows, cols), out_dtype),
        compiler_params=_cparams(("parallel",)),
    )(*arrs)


_WEIGHTS = ('c_ctx', 'ada_w', 'ada_b', 'norm_mix_pre', 'norm_mix_post', 'norm_ffn_pre', 'norm_ffn_post', 'w_in',
            'w_out', 'gla_gate_up', 'gla_gate_b', 'gla_norm', 'ssd_conv_w', 'ssd_conv_b', 'ssd_dt_bias', 'ssd_a_log',
            'ssd_d', 'ssd_norm', 'ret_norm', 'ffn_w13', 'ffn_w2')
_BIG = ('ada_w', 'w_in', 'w_out', 'ffn_w13', 'ffn_w2')
_EXCHANGED = ('w_in', 'w_out', 'ffn_w13', 'ffn_w2')

GLA_CFG = ScanCfg("gla", True, GLA_HEADS, GLA_HEADS, GLA_DK, GLA_DV, 128)
SSD_CFG = ScanCfg("ssd", False, SSD_HEADS, SSD_GROUPS, SSD_STATE, SSD_DV, 128)
RET_CFG = ScanCfg("ret", False, RET_HEADS, RET_HEADS, RET_DH, RET_DH, 128)


def _permute_cols(w):
    parts = [jnp.zeros((w.shape[0], n), w.dtype) if src is None else w[:, src:src + n] for src, n in _PERM]
    return jnp.concatenate(parts, axis=1)


def _unpermute_cols(dw):
    return jnp.concatenate([dw[:, s:s + n] for s, n in _UNPERM], axis=1)


def _rope_tables(t_ctx, t_lat):
    grid_w = 64
    rows = t_lat // grid_w
    row = np.repeat(np.arange(rows), grid_w).astype(np.float32)
    col = np.tile(np.arange(grid_w), rows).astype(np.float32)
    inv = (np.float32(10000.0) ** (-np.arange(16, dtype=np.float32) / np.float32(16))).astype(np.float32)
    ang = np.concatenate([row[:, None] * inv, col[:, None] * inv], axis=-1).astype(np.float32)
    cos, sin = np.cos(ang), np.sin(ang)
    cos_t = np.tile(np.concatenate([cos, cos], -1), (1, RET_HEADS))
    sin_t = np.tile(np.concatenate([-sin, sin], -1), (1, RET_HEADS))
    w = RET_HEADS * RET_DH
    cos_t = np.concatenate([np.ones((t_ctx, w)), cos_t], 0).astype(np.float32)
    sin_t = np.concatenate([np.zeros((t_ctx, w)), sin_t], 0).astype(np.float32)
    return jnp.asarray(cos_t), jnp.asarray(sin_t)


def _pad_lanes(v):
    v = v.reshape(1, -1)
    return jnp.pad(v, ((0, 0), (0, LANES - v.shape[1])))


def _pack(arrs, rows):
    flat = jnp.concatenate([a.reshape(-1) for a in arrs])
    return jnp.pad(flat, (0, rows * LANES - flat.shape[0])).reshape(rows, LANES)


def _unpack(packed, shapes):
    flat, out, i = packed.reshape(-1), [], 0
    for s in shapes:
        n = int(np.prod(s))
        out.append(flat[i:i + n].reshape(s))
        i += n
    return out


def _rows_for(shapes):
    n = sum(int(np.prod(s)) for s in shapes)
    return -(-n // (8 * LANES)) * 8


def _layer_params(a, l, conv_full):
    row = lambda v: v.reshape(1, -1)
    return dict(
        nmp=row(a['norm_mix_pre'][l]), nmpost=row(a['norm_mix_post'][l]), nfp=row(a['norm_ffn_pre'][l]),
        nfpost=row(a['norm_ffn_post'][l]),
        convw8=jnp.pad(conv_full[l], ((0, 3), (0, 0))), convb=row(a['ssd_conv_b'][l]),
        dtb=_pad_lanes(a['ssd_dt_bias'][l]), alog=_pad_lanes(a['ssd_a_log'][l]),
        up=a['gla_gate_up'][l], gbias=a['gla_gate_b'][l][:, None, :],
        gla_nw=row(a['gla_norm'][l]), dsk=row(jnp.repeat(a['ssd_d'][l], SSD_DV)), ssd_nw=row(a['ssd_norm'][l]),
        ret_nw=row(a['ret_norm'][l]))


def _layer_fwd(xs, mod, w, lp, consts, t_ctx, tag):
    cos, sin, retg, pm = consts
    kw = dict(t_ctx=t_ctx)
    p, h1 = nm_matmul(xs, lp['nmp'], mod, w['w_in'], sh=0, sc=1, out_dtype=F32, name="in_proj", **kw)
    u = conv_fwd(p, lp['convw8'], lp['convb'], name="conv_fwd", **kw)
    act, kf, kb, sg = ssd_act_fwd(u, p, lp['dtb'], lp['alog'], name="ssd_act_fwd", **kw)
    qs, ggf, ggb = gla_pre_fwd(p, lp['up'], lp['gbias'], name="gla_pre_fwd", **kw)
    rq, rk = rope_pair([(p, 256, P_RQ // 256)], [(p, 256, P_RK // 256)], cos, sin, transpose=False,
                       name="rope_fwd", **kw)
    scans = dict(
        gla_f=(GLA_CFG, _full(qs), (p, 128, P_GK // 128), (p, 256, P_GV // 256), _full(ggf), False, 0),
        gla_b=(GLA_CFG, _full(qs), (p, 128, P_GK // 128), (p, 256, P_GV // 256), _full(ggb), True, 0),
        ssd_f=(SSD_CFG, (act, 256, 3), _full(kf), (act, 512, 0), _full(sg), False, 0),
        ssd_b=(SSD_CFG, (act, 256, 3), _full(kb), (act, 512, 0), _full(sg), True, SSD_HEADS),
        ret_f=(RET_CFG, _full(rq), _full(rk), (p, 256, P_RV // 256), _full(retg), False, 0),
        ret_b=(RET_CFG, _full(rq), _full(rk), (p, 256, P_RV // 256), _full(retg), True, 0))
    so = {}
    for key, (cfg, q, k, v, g, rev, off) in scans.items():
        so[key] = scan_fwd(cfg, q, k, v, g, t_ctx=t_ctx, reverse=rev, g_off=off)
    post_in = (so['gla_f'][0], so['gla_b'][0], so['ssd_f'][0], so['ssd_b'][0], so['ret_f'][0], so['ret_b'][0],
               p, act, lp['gla_nw'], lp['dsk'], lp['ssd_nw'], lp['ret_nw'], pm)
    y = mix_post_fwd(*post_in, name="mix_post_fwd", **kw)
    xs1, zmix = mm_postnorm(y, w['w_out'], xs, mod, lp['nmpost'], gt=2, swiglu=False, name="out_proj", **kw)
    u13, h2 = nm_matmul(xs1, lp['nfp'], mod, w['ffn_w13'], sh=3, sc=4, out_dtype=BF16, name="ffn_up", **kw)
    xs2, zffn, actf = mm_postnorm(u13, w['ffn_w2'], xs1, mod, lp['nfpost'], gt=5, swiglu=True, name="ffn_down", **kw)
    saved = dict(xs=xs, p=p, h1=h1, u=u, scans=scans, states={k: v[1] for k, v in so.items()}, post_in=post_in,
                 y=y, xs1=xs1, zmix=zmix, u13=u13, h2=h2, zffn=zffn, actf=actf)
    return xs2, saved


def _layer_bwd(dxs, sv, mod, w, lp, consts, t_ctx):
    cos, sin, retg, pm = consts
    kw = dict(t_ctx=t_ctx)
    du13, dzb, dgt2, dnfpost = postnorm_bwd(dxs, sv['zffn'], mod, lp['nfpost'], w['ffn_w2'], sv['u13'], gt=5,
                                            name="ffn_down_bwd", **kw)
    dw2 = grad_matmul(sv['actf'], dzb, name="ffn_w2_grad")
    dxs1, dnfp, dsh2, dsc2 = nm_bwd(du13, w['ffn_w13'], sv['xs1'], lp['nfp'], mod, dxs, sh=3, sc=4,
                                    name="ffn_up_bwd", **kw)
    dw13 = grad_matmul(sv['h2'], du13, name="ffn_w13_grad")
    dy, dzb1, dgt1, dnmpost = postnorm_bwd(dxs1, sv['zmix'], mod, lp['nmpost'], w['w_out'], None, gt=2,
                                           name="out_proj_bwd", **kw)
    dwout = grad_matmul(sv['y'], dzb1, name="w_out_grad")
    (dgo, dso, dro, drg, dz, dgg, dxs_skip, dgla_nw, ddsk, dssd_nw, dret_nw) = mix_post_bwd(
        *sv['post_in'], dy, name="mix_post_bwd", **kw)
    douts = dict(gla=_full(dgo), ssd=_full(dso), ret=_full(dro))
    sb = {}
    for key, (cfg, q, k, v, g, rev, off) in sv['scans'].items():
        sb[key] = scan_bwd(cfg, q, k, v, g, sv['states'][key], douts[key[:3]], t_ctx=t_ctx, reverse=rev, g_off=off)
    p = sv['p']
    drq, drk = rope_pair([_full(sb['ret_f'][0]), _full(sb['ret_b'][0])], [_full(sb['ret_f'][1]), _full(sb['ret_b'][1])],
                         cos, sin, transpose=True, name="rope_bwd", **kw)
    dgq, dlr, dup, dgbias = gla_pre_bwd(p, lp['up'], lp['gbias'], [sb['gla_f'][0], sb['gla_b'][0]],
                                        sb['gla_f'][3], sb['gla_b'][3], name="gla_pre_bwd", **kw)
    du, ddt, ddtb, dalog = ssd_act_bwd(
        sv['u'], p, lp['dtb'], lp['alog'], [sb['ssd_f'][2], sb['ssd_b'][2], dxs_skip], [sb['ssd_f'][0], sb['ssd_b'][0]],
        sb['ssd_f'][1], sb['ssd_b'][1], [sb['ssd_f'][3], sb['ssd_b'][3]], name="ssd_act_bwd", **kw)
    dxbc, dconvw8, dconvb = conv_bwd(p, du, lp['convw8'], name="conv_bwd", **kw)
    dp = dp_assemble([
        (P_XBC, 1024, [dxbc]), (P_RQ, 256, [drq]), (P_RK, 256, [drk]), (P_RV, 256, [sb['ret_f'][2], sb['ret_b'][2]]),
        (P_RG, 256, [dgg]), (P_Z, 512, [dz]), (P_GV, 256, [sb['gla_f'][2], sb['gla_b'][2]]), (P_GR, 256, [drg]),
        (P_GQ, 128, [dgq]), (P_GK, 128, [sb['gla_f'][1], sb['gla_b'][1]]), (P_LR, 128, [dlr]), (P_DT, 128, [ddt])],
        name="dp_assemble", **kw)
    dxs0, dnmp, dsh1, dsc1 = nm_bwd(dp, w['w_in'], sv['xs'], lp['nmp'], mod, dxs1, sh=0, sc=1, name="in_proj_bwd", **kw)
    dwin = grad_matmul(sv['h1'], dp, name="w_in_grad")
    dmod = jnp.concatenate([dsh1, dsc1, dgt1, dsh2, dsc2, dgt2], axis=1)
    small = dict(
        norm_mix_pre=dnmp, norm_mix_post=dnmpost, norm_ffn_pre=dnfp, norm_ffn_post=dnfpost,
        gla_gate_up=dup, gla_gate_b=dgbias[:, 0, :], gla_norm=dgla_nw,
        ssd_conv_w=dconvw8[0:5], ssd_conv_b=dconvb, ssd_dt_bias=ddtb[0, 0:16].reshape(2, SSD_HEADS),
        ssd_a_log=dalog[0, 0:16].reshape(2, SSD_HEADS), ssd_d=ddsk.reshape(SSD_HEADS, SSD_DV).sum(-1),
        ssd_norm=dssd_nw, ret_norm=dret_nw)
    big = dict(w_in=_unpermute_cols(dwin), w_out=dwout, ffn_w13=dw13, ffn_w2=dw2)
    return dxs0, big, dmod, small


def _take_chips(g, m_per):
    return g.reshape(8, m_per, g.shape[1])[0::2]


def kernel(x, c, ctx, c_ctx, ada_w, ada_b, norm_mix_pre, norm_mix_post, norm_ffn_pre, norm_ffn_post, w_in, w_out, gla_gate_up, gla_gate_b, gla_norm, ssd_conv_w, ssd_conv_b, ssd_dt_bias, ssd_a_log, ssd_d, ssd_norm, ret_norm, ffn_w13, ffn_w2, loss_target, m_c_ctx, m_ada_w, m_ada_b, m_norm_mix_pre, m_norm_mix_post, m_norm_ffn_pre, m_norm_ffn_post, m_w_in, m_w_out, m_gla_gate_up, m_gla_gate_b, m_gla_norm, m_ssd_conv_w, m_ssd_conv_b, m_ssd_dt_bias, m_ssd_a_log, m_ssd_d, m_ssd_norm, m_ret_norm, m_ffn_w13, m_ffn_w2, v_c_ctx, v_ada_w, v_ada_b, v_norm_mix_pre, v_norm_mix_post, v_norm_ffn_pre, v_norm_ffn_post, v_w_in, v_w_out, v_gla_gate_up, v_gla_gate_b, v_gla_norm, v_ssd_conv_w, v_ssd_conv_b, v_ssd_dt_bias, v_ssd_a_log, v_ssd_d, v_ssd_norm, v_ret_norm, v_ffn_w13, v_ffn_w2):
    a = dict(locals())
    depth = ada_w.shape[0]
    t_ctx, t_lat = ctx.shape[1], x.shape[1]
    xi, yi, ci = _position()
    dev, chip = 4 * xi + 2 * yi + ci, 2 * xi + yi
    ncol = ada_w.shape[2]

    cw = ssd_conv_w.reshape(-1)
    blk = jnp.concatenate([c[0], cw, jnp.zeros((7 * D - cw.shape[0],), F32)]).reshape(8, D)
    g0 = all_gather8(blk, name="gather_cond").reshape(8, 8, D)
    conv_full = g0[0::2, 1:8].reshape(4, 7 * D)[:, :cw.shape[0]].reshape(4, depth, 5, D // 4)
    conv_full = conv_full.transpose(1, 2, 0, 3).reshape(depth, 5, D)
    cg = jnp.concatenate([g0[:, 0], jnp.broadcast_to(c_ctx[None], (8, D))], axis=0)
    part = ada_fwd(cg, ada_w, name="ada_fwd")
    g1 = _take_chips(all_gather8(part.reshape(depth * 16, ncol), name="gather_mod"), depth * 16)
    mod_all = g1.reshape(4, depth, 16, ncol).transpose(1, 2, 0, 3).reshape(depth, 16, 4 * ncol) + ada_b[:, None, :]
    mods = [jnp.stack([mod_all[l, 8], lax.dynamic_index_in_dim(mod_all[l], dev, 0, keepdims=False)]).reshape(2, 6, D)
            for l in range(depth)]

    halves = [lax.dynamic_slice_in_dim(a[n].astype(BF16), (depth // 2) * ci, depth // 2, axis=0) for n in _EXCHANGED]
    mine = chip_exchange(halves, gather=True, name="weight_gather")
    theirs = pair_exchange(mine, name="weight_pair")

    def layer_weights(l):
        hl, il = l // (depth // 2), l % (depth // 2)
        sh = {n: jnp.where(ci == hl, mine[k][:, il], theirs[k][:, il]) for k, n in enumerate(_EXCHANGED)}
        cols = lambda s: s.transpose(1, 0, 2).reshape(s.shape[1], 4 * s.shape[2])
        rows = lambda s: s.reshape(4 * s.shape[1], s.shape[2])
        return dict(w_in=_permute_cols(cols(sh['w_in'])), w_out=rows(sh['w_out']), ffn_w13=cols(sh['ffn_w13']),
                    ffn_w2=rows(sh['ffn_w2']))

    cos, sin = _rope_tables(t_ctx, t_lat)
    t = t_ctx + t_lat
    gam = np.zeros((t, LANES), np.float32)
    gam[:, :RET_HEADS] = np.log1p(-np.exp2(-5.0 - np.arange(RET_HEADS, dtype=np.float32)))
    pm = np.kron(np.eye(RET_HEADS), np.full((RET_DH, RET_DH), 1.0 / RET_DH)).astype(np.float32)
    consts = (cos, sin, jnp.asarray(gam), jnp.asarray(pm))

    xs = jnp.concatenate([ctx[0], x[0]], axis=0)
    saved, lws, lps = [], [], []
    for l in range(depth):
        lws.append(layer_weights(l))
        lps.append(_layer_params(a, l, conv_full))
        xs, sv = _layer_fwd(xs, mods[l], lws[l], lps[l], consts, t_ctx, l)
        saved.append(sv)
    lvec, dxs = loss_head(xs, loss_target[0], t_ctx=t_ctx, name="loss_head")
    loss = lax.psum(lvec[0, 0], ("x", "y", "c"))

    big = {n: [None] * depth for n in _EXCHANGED}
    small = [None] * depth
    dmods = [None] * depth
    for l in reversed(range(depth)):
        dxs, bg, dmods[l], small[l] = _layer_bwd(dxs, saved[l], mods[l], lws[l], lps[l], consts, t_ctx)
        for n in _EXCHANGED:
            big[n][l] = bg[n]
    grad_x = dxs[t_ctx:][None]

    def shard_major(n, g):
        g = jnp.stack(g)
        if n in ('w_in', 'ffn_w13'):
            return g.reshape(depth, g.shape[1], 4, g.shape[2] // 4).transpose(2, 0, 1, 3)
        return g.reshape(depth, 4, g.shape[1] // 4, g.shape[2]).transpose(1, 0, 2, 3)

    hd = depth // 2
    gsm = [shard_major(n, big[n]).astype(BF16) for n in _EXCHANGED]
    keep = [lax.dynamic_slice_in_dim(g, hd * ci, hd, axis=1) for g in gsm]
    give = [lax.dynamic_slice_in_dim(g, hd * (1 - ci), hd, axis=1) for g in gsm]
    got = pair_exchange(give, name="grad_pair")
    two = lambda g: g.reshape(-1, g.shape[-1])
    pair = [sum_arrays([two(k_), two(g_)], BF16, name="grad_pair_sum").reshape(k_.shape) for k_, g_ in zip(keep, got)]
    from_chips = chip_exchange(pair, gather=False, name="grad_scatter")
    mine_sum = [sum_leading(g.reshape(4, -1, g.shape[-1]), F32, name="grad_chip_sum") for g in from_chips]
    sib_sum = pair_exchange(mine_sum, name="grad_pair_back")
    grads = {}
    for k, n in enumerate(_EXCHANGED):
        both = jnp.where(ci == 0, jnp.concatenate([mine_sum[k], sib_sum[k]]), jnp.concatenate([sib_sum[k], mine_sum[k]]))
        grads[n] = both.reshape(a[n].shape)

    small_names = [n for n in _WEIGHTS if n not in _BIG and n not in ('c_ctx', 'ada_b')]
    small_shapes = [((depth, 5, D) if n == 'ssd_conv_w' else a[n].shape) for n in small_names]
    srows = _rows_for(small_shapes)
    dm = jnp.stack(dmods).reshape(depth, 2, 6 * D)
    mrows = dm.size // LANES
    vec = jnp.concatenate([_pack([jnp.stack([small[l][n] for l in range(depth)]) for n in small_names], srows),
                           dm.reshape(mrows, LANES)], axis=0)
    g2 = all_gather8(vec, name="gather_small").reshape(8, srows + mrows, LANES)
    small_sum = sum_leading(g2[:, :srows], F32, name="small_sum")
    sg = dict(zip(small_names, _unpack(small_sum, small_shapes)))
    dm_all = g2[:, srows:].reshape(8, depth, 2, 6 * D)
    grads['ada_b'] = sum_leading(dm_all.transpose(0, 2, 1, 3).reshape(16, depth * 6 * D // LANES, LANES), F32,
                                 name="ada_b_sum").reshape(depth, 6 * D)
    dm_cols = lax.dynamic_slice_in_dim(dm_all, chip * ncol, ncol, axis=3)
    dmod16 = jnp.concatenate([dm_cols[:, :, 1].transpose(1, 0, 2), dm_cols[:, :, 0].transpose(1, 0, 2)], axis=1)
    grads['ada_w'], dcg = ada_bwd(cg, ada_w, dmod16, name="ada_bwd")
    dcc = sum_leading(dcg[8:16].reshape(8, 1, D), F32, name="c_ctx_rows_sum")
    g3 = all_gather8(jnp.zeros((8, D), F32).at[0:1].set(dcc), name="gather_c_ctx").reshape(8, 8, D)
    grads['c_ctx'] = sum_leading(g3[0::2, 0:1], F32, name="c_ctx_sum").reshape(D)
    for n in small_names:
        grads[n] = sg[n]
    conv_grad_shard = lax.dynamic_slice_in_dim(sg['ssd_conv_w'], chip * (D // 4), D // 4, axis=2)
    grads['ssd_conv_w'] = conv_grad_shard

    delta, new_m, new_v = {}, {}, {}
    for n in _BIG:
        sh = a[n].shape
        two_d = lambda v: v.reshape(-1, sh[-1])
        d_, m_, v_ = adamw(two_d(a[n]), two_d(grads[n]), two_d(a['m_' + n]), two_d(a['v_' + n]), name="adamw_" + n)
        delta[n], new_m[n], new_v[n] = d_.reshape(sh), m_.reshape(sh), v_.reshape(sh)
    packed_names = [n for n in _WEIGHTS if n not in _BIG]
    shapes = [a[n].shape for n in packed_names]
    prow = _rows_for(shapes)
    pk = lambda pre: _pack([(grads[n] if pre == 'g' else a[pre + n]) for n in packed_names], prow)
    d_, m_, v_ = adamw(pk(''), pk('g'), pk('m_'), pk('v_'), name="adamw_small")
    for n, dd, mm, vv in zip(packed_names, _unpack(d_, shapes), _unpack(m_, shapes), _unpack(v_, shapes)):
        delta[n], new_m[n], new_v[n] = dd, mm, vv

    return (loss, grad_x, *[grads[n] for n in _WEIGHTS], *[delta[n] for n in _WEIGHTS],
            *[new_m[n] for n in _WEIGHTS], *[new_v[n] for n in _WEIGHTS])
```

```python
import functools
import math

import numpy as np
import jax
import jax.numpy as jnp
from jax import lax
from jax.experimental import pallas as pl
from jax.experimental.pallas import tpu as pltpu

F32 = jnp.float32
BF16 = jnp.bfloat16
MESH = pl.DeviceIdType.MESH

D = 1024
DEPTH = 4
RMS_EPS = 1e-6
GLA_HEADS, GLA_DK, GLA_DV = 4, 32, 64
SSD_HEADS, SSD_DV, SSD_STATE, SSD_GROUPS = 8, 64, 128, 2
RET_HEADS, RET_DH = 4, 64
FFN_HIDDEN = 2816
IN_COLS = 3376
ADAM_LR, ADAM_B1, ADAM_B2, ADAM_EPS, ADAM_WD, ADAM_STEP = 0.001, 0.9, 0.999, 1e-08, 0.01, 10

P_XBC, P_RQ, P_RK, P_RV, P_RG, P_Z, P_GV, P_GR, P_GQ, P_GK, P_LR, P_DT = (
    0, 1024, 1280, 1536, 1792, 2048, 2560, 2816, 3072, 3200, 3328, 3456)
NP = 3584
_PERM = ((1312, 1024), (2352, 1024), (800, 512), (256, 256), (512, 256), (0, 128), (128, 128), (768, 32),
         (None, 96), (2336, 16), (None, 112))
_UNPERM = ((3072, 128), (3200, 128), (2560, 256), (2816, 256), (3328, 32), (2048, 512), (0, 1024), (3456, 16),
           (1024, 1024))

LANES = 128
VMEM_LIMIT = 56 * 1024 * 1024
TN_CHUNK = 512


def _cparams(sem=None):
    kw = dict(vmem_limit_bytes=VMEM_LIMIT)
    if sem is not None:
        kw["dimension_semantics"] = sem
    return pltpu.CompilerParams(**kw)


def _dot(a, b, ca, cb):
    return lax.dot_general(a.astype(BF16), b.astype(BF16), (((ca,), (cb,)), ((), ())),
                           preferred_element_type=F32)


@jax.custom_vjp
def mm_nn(a, b):
    return _dot(a, b, 1, 0)


@jax.custom_vjp
def mm_nt(a, b):
    return _dot(a, b, 1, 1)


@jax.custom_vjp
def mm_tn(a, b):
    return _dot(a, b, 0, 0)


mm_nn.defvjp(lambda a, b: (mm_nn(a, b), (a, b)), lambda r, g: (mm_nt(g, r[1]), mm_tn(r[0], g)))
mm_nt.defvjp(lambda a, b: (mm_nt(a, b), (a, b)), lambda r, g: (mm_nn(g, r[1]), mm_tn(g, r[0])))
mm_tn.defvjp(lambda a, b: (mm_tn(a, b), (a, b)), lambda r, g: (mm_nt(r[1], g), mm_nn(r[0], g)))


def _hi(a, b, ca, cb):
    return lax.dot_general(a, b, (((ca,), (cb,)), ((), ())), precision=lax.Precision.HIGHEST,
                           preferred_element_type=F32)


def dot_hi_plain(a, b):
    return _hi(a, b, 1, 0)


@jax.custom_vjp
def dot_hi(a, b):
    return _hi(a, b, 1, 0)


dot_hi.defvjp(lambda a, b: (dot_hi(a, b), (a, b)), lambda r, g: (_hi(g, r[1], 1, 1), _hi(r[0].T, g, 1, 0)))


def _sigmoid(x):
    return 1.0 / (1.0 + jnp.exp(-x))


def _silu(x):
    return x * _sigmoid(x)


def _softplus(x):
    return jnp.maximum(x, 0.0) + jnp.log(1.0 + jnp.exp(-jnp.abs(x)))


def _log_sigmoid(x):
    return -_softplus(-x)


def _order_mask(n, reverse):
    r = lax.broadcasted_iota(jnp.int32, (n, n), 0)
    c = lax.broadcasted_iota(jnp.int32, (n, n), 1)
    return ((c >= r) if reverse else (c <= r)).astype(F32)


RET_LOG_GAMMA = tuple(math.log1p(-(2.0 ** (-5.0 - h))) for h in range(RET_HEADS))


def _chunk_vector(q, k, vs, g, sts, *, reverse):
    mask = _order_mask(q.shape[0], reverse)
    cum = dot_hi(mask, g)
    tot = jnp.sum(g, axis=0, keepdims=True)
    mid = 0.5 * tot
    qt = q * jnp.exp(jnp.minimum(cum - mid, 80.0))
    kt = k * jnp.exp(jnp.minimum(mid - cum, 80.0))
    qe = q * jnp.exp(cum)
    ke = k * jnp.exp(tot - cum)
    dec = jnp.exp(tot)
    lane = lax.broadcasted_iota(jnp.int32, (1, LANES), 1)
    outs, new = [], []
    for h in range(GLA_HEADS):
        hm = ((lane >= h * GLA_DK) & (lane < (h + 1) * GLA_DK)).astype(F32)
        a = mm_nt(qt * hm, kt) * mask
        outs.append(mm_nn(a, vs[h]) + mm_nt(qe * hm, sts[h]))
        new.append(sts[h] * dec + mm_tn(vs[h], ke * hm))
    return outs, new


def _chunk_shared(cms, bms, vs, g, dt, sts, *, reverse, g_off):
    n = g.shape[0]
    mask = _order_mask(n, reverse)
    cum = dot_hi(mask, g)
    tot = jnp.sum(g, axis=0, keepdims=True)
    cum_t, dt_t = cum.T, dt.T
    lane = lax.broadcasted_iota(jnp.int32, (1, LANES), 1)
    sub = lax.broadcasted_iota(jnp.int32, (LANES, 1), 0)
    rep = SSD_HEADS // SSD_GROUPS
    cb = [mm_nt(cms[i], bms[i]) for i in range(SSD_GROUPS)]
    outs, new = [], []
    for h in range(SSD_HEADS):
        pl_, ps_ = (lane == g_off + h).astype(F32), (sub == g_off + h).astype(F32)
        gh = jnp.sum(cum * pl_, axis=1, keepdims=True)
        th = jnp.sum(tot * pl_, axis=1, keepdims=True)
        dt_col = jnp.sum(dt * pl_, axis=1, keepdims=True)
        g_row = jnp.sum(cum_t * ps_, axis=0, keepdims=True)
        dt_row = jnp.sum(dt_t * ps_, axis=0, keepdims=True)
        a = cb[h // rep] * (jnp.exp(jnp.minimum(gh - g_row, 0.0)) * mask * dt_row)
        outs.append(mm_nn(a, vs[h]) + mm_nt(cms[h // rep] * jnp.exp(gh), sts[h]))
        new.append(sts[h] * jnp.exp(th) + mm_tn(vs[h] * (dt_col * jnp.exp(th - gh)), bms[h // rep]))
    return outs, new


def _chunk_const(qs, ks, vs, sts, *, reverse):
    n = qs[0].shape[0]
    r = lax.broadcasted_iota(jnp.int32, (n, n), 0)
    c = lax.broadcasted_iota(jnp.int32, (n, n), 1)
    dist = ((c - r) if reverse else (r - c)).astype(F32)
    row = lax.broadcasted_iota(jnp.int32, (n, 1), 0).astype(F32)
    seen = (n - row) if reverse else (row + 1.0)
    outs, new = [], []
    for h in range(RET_HEADS):
        gm = RET_LOG_GAMMA[h]
        dec = jnp.where(dist >= 0.0, jnp.exp(gm * dist), 0.0)
        a = mm_nt(qs[h], ks[h]) * dec
        outs.append(mm_nn(a, vs[h]) + mm_nt(qs[h] * jnp.exp(gm * seen), sts[h]))
        new.append(sts[h] * math.exp(gm * n) + mm_tn(vs[h], ks[h] * jnp.exp(gm * (n - seen))))
    return outs, new


class ScanCfg:
    def __init__(self, name, fn, heads, dv, st_k, chunk, parts):
        self.name, self.fn, self.heads, self.dv, self.st_k, self.chunk, self.parts = name, fn, heads, dv, st_k, chunk, parts

    def width(self, i):
        return LANES if self.parts[i] is None else self.parts[i][0] * self.parts[i][1]


def _scan_load(cfg, refs):
    vals = []
    for rf, part in zip(refs, cfg.parts):
        if part is None:
            vals.append(rf[...].astype(F32))
        else:
            vals.append([rf[:, i * part[1]:(i + 1) * part[1]].astype(F32) for i in range(part[0])])
    return vals


def _scan_store(cfg, refs, grads):
    for rf, part, g in zip(refs, cfg.parts, grads):
        if part is None:
            rf[...] = g
        else:
            for i in range(part[0]):
                rf[:, i * part[1]:(i + 1) * part[1]] = g[i]


def _chunk_of_step(s, n, nc, reverse):
    if not reverse:
        return s
    return jnp.where(s < nc, nc - 1 - s, n + nc - 1 - s)


def _arr_spec(a, c, pick):
    arr, w, cb = a
    return pl.BlockSpec((c, w), lambda s: (pick(s), cb))


def scan_fwd(cfg, ins, *, t_ctx, reverse, **kw):
    t = ins[0][0].shape[0]
    c, h, ni = cfg.chunk, cfg.heads, len(ins)
    n, nc = t // c, t_ctx // c
    pick = lambda s: _chunk_of_step(s, n, nc, reverse)

    def body(*refs):
        o_ref, st_ref, state = refs[ni:]

        @pl.when(pl.program_id(0) == 0)
        def _():
            state[...] = jnp.zeros_like(state)

        st_ref[...] = state[...]
        outs, new = cfg.fn(*_scan_load(cfg, refs[:ni]), [state[i] for i in range(h)], reverse=reverse, **kw)
        for i in range(h):
            o_ref[:, i * cfg.dv:(i + 1) * cfg.dv] = outs[i]
            state[i] = new[i]

    return pl.pallas_call(
        body, name=f"scan_fwd_{cfg.name}_{'b' if reverse else 'f'}", grid=(n,),
        in_specs=[_arr_spec(a, c, pick) for a in ins],
        out_specs=[pl.BlockSpec((c, h * cfg.dv), lambda s: (pick(s), 0)),
                   pl.BlockSpec((None, h, cfg.dv, cfg.st_k), lambda s: (pick(s), 0, 0, 0))],
        out_shape=[jax.ShapeDtypeStruct((t, h * cfg.dv), F32),
                   jax.ShapeDtypeStruct((n, h, cfg.dv, cfg.st_k), F32)],
        scratch_shapes=[pltpu.VMEM((h, cfg.dv, cfg.st_k), F32)],
        compiler_params=_cparams(("arbitrary",)),
    )(*[a[0] for a in ins])


def scan_bwd(cfg, ins, states, do, *, t_ctx, reverse, **kw):
    t = ins[0][0].shape[0]
    c, h, ni = cfg.chunk, cfg.heads, len(ins)
    n, nc = t // c, t_ctx // c
    pick = lambda s: _chunk_of_step(n - 1 - s, n, nc, reverse)

    def body(*refs):
        st_ref, do_ref = refs[ni:ni + 2]
        grad_refs, dstate = refs[ni + 2:-1], refs[-1]

        @pl.when(pl.program_id(0) == 0)
        def _():
            dstate[...] = jnp.zeros_like(dstate)

        dos = [do_ref[:, i * cfg.dv:(i + 1) * cfg.dv] for i in range(h)]
        _, vjp = jax.vjp(functools.partial(cfg.fn, reverse=reverse, **kw), *_scan_load(cfg, refs[:ni]),
                         [st_ref[i] for i in range(h)])
        grads = vjp((dos, [dstate[i] for i in range(h)]))
        _scan_store(cfg, grad_refs, grads[:-1])
        for i in range(h):
            dstate[i] = grads[-1][i]

    row = lambda w: pl.BlockSpec((c, w), lambda s: (pick(s), 0))
    return pl.pallas_call(
        body, name=f"scan_bwd_{cfg.name}_{'b' if reverse else 'f'}", grid=(n,),
        in_specs=[_arr_spec(a, c, pick) for a in ins]
        + [pl.BlockSpec((None, h, cfg.dv, cfg.st_k), lambda s: (pick(s), 0, 0, 0)), _arr_spec(do, c, pick)],
        out_specs=[row(cfg.width(i)) for i in range(ni)],
        out_shape=[jax.ShapeDtypeStruct((t, cfg.width(i)), F32) for i in range(ni)],
        scratch_shapes=[pltpu.VMEM((h, cfg.dv, cfg.st_k), F32)],
        compiler_params=_cparams(("arbitrary",)),
    )(*[a[0] for a in ins], states, do[0])


def _rb(t_ctx):
    return min(256, t_ctx)


class _Rows:
    def __init__(self, t, t_ctx):
        self.rb = _rb(t_ctx)
        self.nb, self.ncb = t // self.rb, t_ctx // self.rb

    def seg(self, i):
        return jnp.where(i >= self.ncb, 1, 0)

    def rows(self, w, cb=0):
        return pl.BlockSpec((self.rb, w), lambda i: (i, cb))

    def arr(self, a):
        return self.rows(a[1], a[2])

    def const(self, *shape):
        return pl.BlockSpec(shape, lambda i: (0,) * len(shape))

    def per_seg(self, *shape):
        return pl.BlockSpec((None,) + shape, lambda i: (self.seg(i),) + (0,) * len(shape))

    def first(self):
        return pl.program_id(0) == 0

    def seg_first(self):
        i = pl.program_id(0)
        return (i == 0) | (i == self.ncb)


def _acc(ref, val, first):
    @pl.when(first)
    def _():
        ref[...] = val

    @pl.when(jnp.logical_not(first))
    def _():
        ref[...] += val


def _full(a):
    return (a, a.shape[1], 0)


def _norm_mod(x, nw, sh, sc):
    y = x * lax.rsqrt(jnp.mean(x * x, axis=-1, keepdims=True) + RMS_EPS)
    return (y * nw) * (1.0 + sc) + sh


def _gated_norm(z, gt, pw):
    y = z * lax.rsqrt(jnp.mean(z * z, axis=-1, keepdims=True) + RMS_EPS)
    return gt * (y * pw)


def _swiglu(g, u):
    return _silu(g) * u


def nm_matmul(x, nw, mod, w, *, sh, sc, t_ctx, out_dtype, name):
    t, n = x.shape[0], w.shape[1]
    r = _Rows(t, t_ctx)

    def body(x_ref, nw_ref, mod_ref, w_ref, o_ref, h_ref):
        h = _norm_mod(x_ref[...], nw_ref[...], mod_ref[sh:sh + 1, :], mod_ref[sc:sc + 1, :]).astype(BF16)
        h_ref[...] = h
        for j in range(n // TN_CHUNK):
            sl = slice(j * TN_CHUNK, (j + 1) * TN_CHUNK)
            o_ref[:, sl] = _dot(h, w_ref[:, sl], 1, 0).astype(out_dtype)

    return pl.pallas_call(
        body, name=name, grid=(r.nb,),
        in_specs=[r.rows(D), r.const(1, D), r.per_seg(6, D), r.const(D, n)],
        out_specs=[r.rows(n), r.rows(D)],
        out_shape=[jax.ShapeDtypeStruct((t, n), out_dtype), jax.ShapeDtypeStruct((t, D), BF16)],
        compiler_params=_cparams(("arbitrary",)),
    )(x, nw, mod, w)


def nm_bwd(dout, w, x, nw, mod, dres, *, sh, sc, t_ctx, name):
    t, n = x.shape[0], w.shape[1]
    r = _Rows(t, t_ctx)

    def body(do_ref, w_ref, x_ref, nw_ref, mod_ref, dres_ref, dx_ref, dnw_ref, dsh_ref, dsc_ref):
        dh = jnp.zeros((r.rb, D), F32)
        for j in range(n // TN_CHUNK):
            sl = slice(j * TN_CHUNK, (j + 1) * TN_CHUNK)
            dh = dh + _dot(do_ref[:, sl], w_ref[:, sl], 1, 1)
        _, vjp = jax.vjp(_norm_mod, x_ref[...], nw_ref[...], mod_ref[sh:sh + 1, :], mod_ref[sc:sc + 1, :])
        dx, dnw, dsh, dsc = vjp(dh)
        dx_ref[...] = dx + dres_ref[...]
        _acc(dnw_ref, dnw, r.first())
        _acc(dsh_ref, dsh, r.seg_first())
        _acc(dsc_ref, dsc, r.seg_first())

    return pl.pallas_call(
        body, name=name, grid=(r.nb,),
        in_specs=[r.rows(n), r.const(D, n), r.rows(D), r.const(1, D), r.per_seg(6, D), r.rows(D)],
        out_specs=[r.rows(D), r.const(1, D), r.per_seg(1, D), r.per_seg(1, D)],
        out_shape=[jax.ShapeDtypeStruct((t, D), F32), jax.ShapeDtypeStruct((1, D), F32),
                   jax.ShapeDtypeStruct((2, 1, D), F32), jax.ShapeDtypeStruct((2, 1, D), F32)],
        compiler_params=_cparams(("arbitrary",)),
    )(dout, w, x, nw, mod, dres)


def mm_postnorm(a, w, xres, mod, pw, *, gt, swiglu, t_ctx, name):
    t, k = xres.shape[0], w.shape[0]
    r = _Rows(t, t_ctx)

    def body(a_ref, w_ref, x_ref, mod_ref, pw_ref, xn_ref, z_ref, *act_ref):
        if swiglu:
            act = _swiglu(a_ref[:, :k].astype(F32), a_ref[:, k:].astype(F32)).astype(BF16)
            act_ref[0][...] = act
        else:
            act = a_ref[...]
        z = _dot(act, w_ref[...], 1, 0)
        z_ref[...] = z
        xn_ref[...] = x_ref[...] + _gated_norm(z, mod_ref[gt:gt + 1, :], pw_ref[...])

    outs = [jax.ShapeDtypeStruct((t, D), F32), jax.ShapeDtypeStruct((t, D), F32)]
    ospecs = [r.rows(D), r.rows(D)]
    if swiglu:
        outs.append(jax.ShapeDtypeStruct((t, k), BF16))
        ospecs.append(r.rows(k))
    return pl.pallas_call(
        body, name=name, grid=(r.nb,),
        in_specs=[r.rows(a.shape[1]), r.const(k, D), r.rows(D), r.per_seg(6, D), r.const(1, D)],
        out_specs=ospecs, out_shape=outs,
        compiler_params=_cparams(("arbitrary",)),
    )(a, w, xres, mod, pw)


def postnorm_bwd(dxn, z, mod, pw, w, u13, *, gt, t_ctx, name):
    t, k = z.shape[0], w.shape[0]
    r = _Rows(t, t_ctx)
    swiglu = u13 is not None

    def body(dxn_ref, z_ref, mod_ref, pw_ref, w_ref, *rest):
        if swiglu:
            u_ref, da_ref, dz_ref, dgt_ref, dpw_ref = rest
        else:
            da_ref, dz_ref, dgt_ref, dpw_ref = rest
        _, vjp = jax.vjp(_gated_norm, z_ref[...], mod_ref[gt:gt + 1, :], pw_ref[...])
        dz, dgt, dpw = vjp(dxn_ref[...])
        dzb = dz.astype(BF16)
        dz_ref[...] = dzb
        da = _dot(dzb, w_ref[...], 1, 1)
        if swiglu:
            _, vjp2 = jax.vjp(_swiglu, u_ref[:, :k].astype(F32), u_ref[:, k:].astype(F32))
            dg, du = vjp2(da)
            da_ref[:, :k] = dg.astype(BF16)
            da_ref[:, k:] = du.astype(BF16)
        else:
            da_ref[...] = da
        _acc(dgt_ref, dgt, r.seg_first())
        _acc(dpw_ref, dpw, r.first())

    ins = [dxn, z, mod, pw, w] + ([u13] if swiglu else [])
    in_specs = [r.rows(D), r.rows(D), r.per_seg(6, D), r.const(1, D), r.const(k, D)] + ([r.rows(2 * k)] if swiglu else [])
    da_shape = jax.ShapeDtypeStruct((t, 2 * k), BF16) if swiglu else jax.ShapeDtypeStruct((t, k), F32)
    return pl.pallas_call(
        body, name=name, grid=(r.nb,), in_specs=in_specs,
        out_specs=[r.rows(2 * k if swiglu else k), r.rows(D), r.per_seg(1, D), r.const(1, D)],
        out_shape=[da_shape, jax.ShapeDtypeStruct((t, D), BF16), jax.ShapeDtypeStruct((2, 1, D), F32),
                   jax.ShapeDtypeStruct((1, D), F32)],
        compiler_params=_cparams(("arbitrary",)),
    )(*ins)


def grad_matmul(a, b, *, name):
    t, k = a.shape
    n = b.shape[1]
    tt = t // 4
    tn = TN_CHUNK if n % TN_CHUNK == 0 else n
    tk = k if k <= 1024 else k // 2

    def body(a_ref, b_ref, o_ref):
        @pl.when(pl.program_id(2) == 0)
        def _():
            o_ref[...] = jnp.zeros_like(o_ref)

        o_ref[...] += _dot(a_ref[...], b_ref[...], 0, 0)

    return pl.pallas_call(
        body, name=name, grid=(k // tk, n // tn, t // tt),
        in_specs=[pl.BlockSpec((tt, tk), lambda i, j, s: (s, i)), pl.BlockSpec((tt, tn), lambda i, j, s: (s, j))],
        out_specs=pl.BlockSpec((tk, tn), lambda i, j, s: (i, j)),
        out_shape=jax.ShapeDtypeStruct((k, n), F32),
        compiler_params=_cparams(("parallel", "parallel", "arbitrary")),
    )(a, b)


def _shifted(cur, prev, nxt, d, has_prev, has_next):
    n = cur.shape[0]
    row = lax.broadcasted_iota(jnp.int32, cur.shape, 0)
    if d == 0:
        return cur
    if d < 0:
        return jnp.where(row < -d, pltpu.roll(prev, -d, 0) * has_prev, pltpu.roll(cur, -d, 0))
    return jnp.where(row >= n - d, pltpu.roll(nxt, n - d, 0) * has_next, pltpu.roll(cur, n - d, 0))


def _halo_specs(r, w, cb):
    return [pl.BlockSpec((r.rb, w), lambda i: (jnp.maximum(i - 1, 0), cb)),
            pl.BlockSpec((r.rb, w), lambda i: (i, cb)),
            pl.BlockSpec((r.rb, w), lambda i: (jnp.minimum(i + 1, r.nb - 1), cb))]


def _halo_flags(r):
    i = pl.program_id(0)
    has_prev = ((i != 0) & (i != r.ncb)).astype(F32)
    has_next = ((i != r.ncb - 1) & (i != r.nb - 1)).astype(F32)
    return has_prev, has_next


def conv_fwd(p, w8, b, *, t_ctx, name):
    t = p.shape[0]
    r = _Rows(t, t_ctx)
    cw = 1024

    def body(pp_ref, pc_ref, pn_ref, w_ref, b_ref, u_ref):
        hp, hn = _halo_flags(r)
        prev, cur, nxt = pp_ref[...], pc_ref[...], pn_ref[...]
        acc = jnp.broadcast_to(b_ref[...], cur.shape)
        for kk in range(5):
            acc = acc + w_ref[kk:kk + 1, :] * _shifted(cur, prev, nxt, kk - 2, hp, hn)
        u_ref[...] = acc

    return pl.pallas_call(
        body, name=name, grid=(r.nb,),
        in_specs=_halo_specs(r, cw, P_XBC // cw) + [r.const(8, cw), r.const(1, cw)],
        out_specs=r.rows(cw), out_shape=jax.ShapeDtypeStruct((t, cw), F32),
        compiler_params=_cparams(("arbitrary",)),
    )(p, p, p, w8, b)


def conv_bwd(p, du, w8, *, t_ctx, name):
    t = p.shape[0]
    r = _Rows(t, t_ctx)
    cw = 1024

    def body(pp_ref, pc_ref, pn_ref, dp_ref, dc_ref, dn_ref, w_ref, dx_ref, dw_ref, db_ref):
        hp, hn = _halo_flags(r)
        prev, cur, nxt = pp_ref[...], pc_ref[...], pn_ref[...]
        dprev, dcur, dnxt = dp_ref[...], dc_ref[...], dn_ref[...]
        acc = jnp.zeros_like(dcur)
        for kk in range(5):
            acc = acc + w_ref[kk:kk + 1, :] * _shifted(dcur, dprev, dnxt, 2 - kk, hp, hn)
        dx_ref[...] = acc
        first = r.first()
        for kk in range(5):
            _acc(dw_ref.at[kk:kk + 1, :], jnp.sum(_shifted(cur, prev, nxt, kk - 2, hp, hn) * dcur, axis=0, keepdims=True),
                 first)
        _acc(dw_ref.at[5:8, :], jnp.zeros((3, cw), F32), first)
        _acc(db_ref, jnp.sum(dcur, axis=0, keepdims=True), first)

    return pl.pallas_call(
        body, name=name, grid=(r.nb,),
        in_specs=_halo_specs(r, cw, P_XBC // cw) + _halo_specs(r, cw, 0) + [r.const(8, cw)],
        out_specs=[r.rows(cw), r.const(8, cw), r.const(1, cw)],
        out_shape=[jax.ShapeDtypeStruct((t, cw), F32), jax.ShapeDtypeStruct((8, cw), F32),
                   jax.ShapeDtypeStruct((1, cw), F32)],
        compiler_params=_cparams(("arbitrary",)),
    )(p, p, p, du, du, du, w8)


def _ssd_act(u_x, u_b0, u_b1, u_c0, u_c1, dtraw, dtb, alog):
    dt = _softplus(dtraw + dtb)
    return _silu(u_x), _silu(u_b0), _silu(u_b1), _silu(u_c0), _silu(u_c1), dt, dt * (-jnp.exp(alog))


def _ssd_act_inputs(u_ref, dt_ref, dtb_ref, alog_ref):
    return (u_ref[:, 0:512], u_ref[:, 512:640], u_ref[:, 640:768], u_ref[:, 768:896], u_ref[:, 896:1024],
            dt_ref[...], dtb_ref[...], alog_ref[...])


_SSD_ACT_OFFSETS = ((0, 512), (512, 128), (640, 128), (768, 128), (896, 128))


def ssd_act_fwd(u, p, dtb, alog, *, t_ctx, name):
    t = u.shape[0]
    r = _Rows(t, t_ctx)

    def body(u_ref, dt_ref, dtb_ref, alog_ref, act_ref, dto_ref, g_ref):
        *pieces, dt, g = _ssd_act(*_ssd_act_inputs(u_ref, dt_ref, dtb_ref, alog_ref))
        for (off, w), val in zip(_SSD_ACT_OFFSETS, pieces):
            act_ref[:, off:off + w] = val
        dto_ref[...] = dt
        g_ref[...] = g

    return pl.pallas_call(
        body, name=name, grid=(r.nb,),
        in_specs=[r.rows(1024), r.rows(LANES, P_DT // LANES), r.const(1, LANES), r.const(1, LANES)],
        out_specs=[r.rows(1024), r.rows(LANES), r.rows(LANES)],
        out_shape=[jax.ShapeDtypeStruct((t, 1024), F32), jax.ShapeDtypeStruct((t, LANES), F32),
                   jax.ShapeDtypeStruct((t, LANES), F32)],
        compiler_params=_cparams(("arbitrary",)),
    )(u, p, dtb, alog)


def ssd_act_bwd(u, p, dtb, alog, dxs_list, db_list, dc_list, ddt_list, dg_list, *, t_ctx, name):
    t = u.shape[0]
    r = _Rows(t, t_ctx)
    lists = (dxs_list, db_list, dc_list, ddt_list, dg_list)
    widths = (512, 256, 256, LANES, LANES)

    def body(*refs):
        u_ref, dt_ref, dtb_ref, alog_ref = refs[:4]
        i, sums = 4, []
        for lst in lists:
            sums.append(sum(rf[...] for rf in refs[i:i + len(lst)]))
            i += len(lst)
        du_ref, ddt_ref, ddtb_ref, dalog_ref = refs[i:]
        dxs, dbm, dcm, ddt_ct, dg_ct = sums
        _, vjp = jax.vjp(_ssd_act, *_ssd_act_inputs(u_ref, dt_ref, dtb_ref, alog_ref))
        *dpieces, ddt, ddtb, dalog = vjp((dxs, dbm[:, 0:128], dbm[:, 128:256], dcm[:, 0:128], dcm[:, 128:256],
                                          ddt_ct, dg_ct))
        for (off, w), val in zip(_SSD_ACT_OFFSETS, dpieces):
            du_ref[:, off:off + w] = val
        ddt_ref[...] = ddt
        _acc(ddtb_ref, ddtb, r.first())
        _acc(dalog_ref, dalog, r.first())

    ins = [u, p, dtb, alog] + [a for lst in lists for a in lst]
    in_specs = ([r.rows(1024), r.rows(LANES, P_DT // LANES), r.const(1, LANES), r.const(1, LANES)]
                + [r.rows(w) for lst, w in zip(lists, widths) for _ in lst])
    return pl.pallas_call(
        body, name=name, grid=(r.nb,), in_specs=in_specs,
        out_specs=[r.rows(1024), r.rows(LANES), r.const(1, LANES), r.const(1, LANES)],
        out_shape=[jax.ShapeDtypeStruct((t, 1024), F32), jax.ShapeDtypeStruct((t, LANES), F32),
                   jax.ShapeDtypeStruct((1, LANES), F32), jax.ShapeDtypeStruct((1, LANES), F32)],
        compiler_params=_cparams(("arbitrary",)),
    )(*ins)


GLA_RANK = 16


def _gla_pre(q, lr, ups_f, ups_b, bf, bb):
    lane = lax.broadcasted_iota(jnp.int32, (1, LANES), 1)
    zf, zb = bf, bb
    for i in range(GLA_RANK):
        zf = zf + jnp.sum(lr * (lane == i).astype(F32), axis=1, keepdims=True) * ups_f[i]
        zb = zb + jnp.sum(lr * (lane == GLA_RANK + i).astype(F32), axis=1, keepdims=True) * ups_b[i]
    return q * (GLA_DK ** -0.5), _log_sigmoid(zf) * (1.0 / 16.0), _log_sigmoid(zb) * (1.0 / 16.0)


def _gla_pre_inputs(q_ref, lr_ref, up_ref, b_ref):
    rows = lambda d: [up_ref[d, i:i + 1, :] for i in range(GLA_RANK)]
    return q_ref[...], lr_ref[...], rows(0), rows(1), b_ref[0], b_ref[1]


def _gla_pre_specs(r):
    return [r.rows(LANES, P_GQ // LANES), r.rows(LANES, P_LR // LANES), r.const(2, GLA_RANK, LANES),
            r.const(2, 1, LANES)]


def gla_pre_fwd(p, up, gbias, *, t_ctx, name):
    t = p.shape[0]
    r = _Rows(t, t_ctx)

    def body(q_ref, lr_ref, up_ref, b_ref, qs_ref, gf_ref, gb_ref):
        qs, gf, gb = _gla_pre(*_gla_pre_inputs(q_ref, lr_ref, up_ref, b_ref))
        qs_ref[...] = qs
        gf_ref[...] = gf
        gb_ref[...] = gb

    sd = jax.ShapeDtypeStruct((t, LANES), F32)
    return pl.pallas_call(
        body, name=name, grid=(r.nb,), in_specs=_gla_pre_specs(r),
        out_specs=[r.rows(LANES)] * 3, out_shape=[sd, sd, sd],
        compiler_params=_cparams(("arbitrary",)),
    )(p, p, up, gbias)


def gla_pre_bwd(p, up, gbias, dq_list, dgf, dgb, *, t_ctx, name):
    t = p.shape[0]
    r = _Rows(t, t_ctx)
    nq = len(dq_list)

    def body(*refs):
        q_ref, lr_ref, up_ref, b_ref = refs[:4]
        dq = sum(rf[...] for rf in refs[4:4 + nq])
        dgf_ref, dgb_ref, dqo_ref, dlr_ref, dup_ref, db_ref = refs[4 + nq:]
        _, vjp = jax.vjp(_gla_pre, *_gla_pre_inputs(q_ref, lr_ref, up_ref, b_ref))
        dqo, dlr, dups_f, dups_b, dbf, dbb = vjp((dq, dgf_ref[...], dgb_ref[...]))
        dqo_ref[...] = dqo
        dlr_ref[...] = dlr

        def write(add):
            for d, (dups, dbias) in enumerate(((dups_f, dbf), (dups_b, dbb))):
                for i in range(GLA_RANK):
                    dup_ref[d, i:i + 1, :] = dups[i] + (dup_ref[d, i:i + 1, :] if add else 0.0)
                db_ref[d] = dbias + (db_ref[d] if add else 0.0)

        first = r.first()
        pl.when(first)(lambda: write(False))
        pl.when(jnp.logical_not(first))(lambda: write(True))

    sd = jax.ShapeDtypeStruct((t, LANES), F32)
    return pl.pallas_call(
        body, name=name, grid=(r.nb,), in_specs=_gla_pre_specs(r) + [r.rows(LANES)] * (nq + 2),
        out_specs=[r.rows(LANES), r.rows(LANES), r.const(2, GLA_RANK, LANES), r.const(2, 1, LANES)],
        out_shape=[sd, sd, jax.ShapeDtypeStruct((2, GLA_RANK, LANES), F32), jax.ShapeDtypeStruct((2, 1, LANES), F32)],
        compiler_params=_cparams(("arbitrary",)),
    )(p, p, up, gbias, *dq_list, dgf, dgb)


def _swap_halves(v):
    lane = lax.broadcasted_iota(jnp.int32, v.shape, 1)
    w = v.shape[1]
    return jnp.where((lane & 63) >= 32, pltpu.roll(v, 32, 1), pltpu.roll(v, w - 32, 1))


def rope_pair(q_list, k_list, cos, sin, *, transpose, t_ctx, name):
    t = cos.shape[0]
    r = _Rows(t, t_ctx)
    nq, nk = len(q_list), len(k_list)
    w = RET_HEADS * RET_DH

    def body(*refs):
        q = sum(rf[...] for rf in refs[:nq])
        k = sum(rf[...] for rf in refs[nq:nq + nk])
        cos_ref, sin_ref, qo_ref, ko_ref = refs[nq + nk:]
        cs, sn = cos_ref[...], sin_ref[...]
        if transpose:
            rot = lambda v: v * cs + _swap_halves(v * sn)
        else:
            rot = lambda v: v * cs + _swap_halves(v) * sn
        qo_ref[...] = rot(q) * (RET_DH ** -0.5)
        ko_ref[...] = rot(k)

    sd = jax.ShapeDtypeStruct((t, w), F32)
    return pl.pallas_call(
        body, name=name, grid=(r.nb,),
        in_specs=[r.arr(a) for a in list(q_list) + list(k_list)] + [r.rows(w), r.rows(w)],
        out_specs=[r.rows(w), r.rows(w)], out_shape=[sd, sd],
        compiler_params=_cparams(("arbitrary",)),
    )(*[a[0] for a in list(q_list) + list(k_list)], cos, sin)


def _gla_post(o_f, o_b, r_gate, nw, pm):
    o = o_f + o_b
    ms = dot_hi(o * o, pm)
    return o * lax.rsqrt(ms + RMS_EPS) * nw * _silu(r_gate)


def _ssd_post(o_f, o_b, xs, z, dsk, nw):
    y = (o_f + o_b + dsk * xs) * _silu(z)
    return y * lax.rsqrt(jnp.mean(y * y, axis=-1, keepdims=True) + RMS_EPS) * nw


def _ret_post(o_f, o_b, gate, nw, pm):
    o = o_f + o_b
    xc = o - dot_hi(o, pm)
    var = dot_hi(xc * xc, pm)
    return xc * lax.rsqrt(var + RMS_EPS) * nw * _silu(gate)


def _mix_post_specs(r):
    return [r.rows(256), r.rows(256), r.rows(512), r.rows(512), r.rows(256), r.rows(256),
            r.rows(256, P_GR // 256), r.rows(512, P_Z // 512), r.rows(256, P_RG // 256),
            r.rows(512, 0),
            r.const(1, 256), r.const(1, 512), r.const(1, 512), r.const(1, 256), r.const(256, 256)]


def mix_post_fwd(go_f, go_b, so_f, so_b, ro_f, ro_b, p, act, gla_nw, dsk, ssd_nw, ret_nw, pm, *, t_ctx, name):
    t = p.shape[0]
    r = _Rows(t, t_ctx)

    def body(gf, gb, sf, sb, rf, rbk, rg, z, gg, xs, gnw, dk, snw, rnw, pmr, y_ref):
        y_ref[:, 0:256] = _gla_post(gf[...], gb[...], rg[...], gnw[...], pmr[...]).astype(BF16)
        y_ref[:, 256:768] = _ssd_post(sf[...], sb[...], xs[...], z[...], dk[...], snw[...]).astype(BF16)
        y_ref[:, 768:1024] = _ret_post(rf[...], rbk[...], gg[...], rnw[...], pmr[...]).astype(BF16)

    return pl.pallas_call(
        body, name=name, grid=(r.nb,), in_specs=_mix_post_specs(r),
        out_specs=r.rows(D), out_shape=jax.ShapeDtypeStruct((t, D), BF16),
        compiler_params=_cparams(("arbitrary",)),
    )(go_f, go_b, so_f, so_b, ro_f, ro_b, p, p, p, act, gla_nw, dsk, ssd_nw, ret_nw, pm)


def mix_post_bwd(go_f, go_b, so_f, so_b, ro_f, ro_b, p, act, gla_nw, dsk, ssd_nw, ret_nw, pm, dy, *, t_ctx, name):
    t = p.shape[0]
    r = _Rows(t, t_ctx)

    def body(gf, gb, sf, sb, rf, rbk, rg, z, gg, xs, gnw, dk, snw, rnw, pmr, dy_ref,
             dgo, dso, dro, drg, dz, dgg, dxs, dgnw, ddk, dsnw, drnw):
        first = r.first()
        pmv = pmr[...]
        _, vjp = jax.vjp(lambda a, b, c, d: _gla_post(a, b, c, d, pmv), gf[...], gb[...], rg[...], gnw[...])
        a, _, c, d = vjp(dy_ref[:, 0:256])
        dgo[...] = a
        drg[...] = c
        _acc(dgnw, d, first)
        _, vjp = jax.vjp(_ssd_post, sf[...], sb[...], xs[...], z[...], dk[...], snw[...])
        a, _, c, d, e, f = vjp(dy_ref[:, 256:768])
        dso[...] = a
        dxs[...] = c
        dz[...] = d
        _acc(ddk, e, first)
        _acc(dsnw, f, first)
        _, vjp = jax.vjp(lambda a, b, c, d: _ret_post(a, b, c, d, pmv), rf[...], rbk[...], gg[...], rnw[...])
        a, _, c, d = vjp(dy_ref[:, 768:1024])
        dro[...] = a
        dgg[...] = c
        _acc(drnw, d, first)

    sd = lambda w: jax.ShapeDtypeStruct((t, w), F32)
    sp = lambda w: jax.ShapeDtypeStruct((1, w), F32)
    return pl.pallas_call(
        body, name=name, grid=(r.nb,), in_specs=_mix_post_specs(r) + [r.rows(D)],
        out_specs=[r.rows(256), r.rows(512), r.rows(256), r.rows(256), r.rows(512), r.rows(256), r.rows(512),
                   r.const(1, 256), r.const(1, 512), r.const(1, 512), r.const(1, 256)],
        out_shape=[sd(256), sd(512), sd(256), sd(256), sd(512), sd(256), sd(512), sp(256), sp(512), sp(512), sp(256)],
        compiler_params=_cparams(("arbitrary",)),
    )(go_f, go_b, so_f, so_b, ro_f, ro_b, p, p, p, act, gla_nw, dsk, ssd_nw, ret_nw, pm, dy)


def dp_assemble(pieces, *, t_ctx, name):
    t = pieces[0][2][0].shape[0]
    r = _Rows(t, t_ctx)
    flat = [a for _, _, arrs in pieces for a in arrs]

    def body(*refs):
        o_ref = refs[-1]
        i = 0
        for start, w, arrs in pieces:
            tot = sum(refs[i + j][...] for j in range(len(arrs)))
            i += len(arrs)
            o_ref[:, start:start + w] = tot.astype(BF16)

    return pl.pallas_call(
        body, name=name, grid=(r.nb,),
        in_specs=[r.rows(w) for _, w, arrs in pieces for _ in arrs],
        out_specs=r.rows(NP), out_shape=jax.ShapeDtypeStruct((t, NP), BF16),
        compiler_params=_cparams(("arbitrary",)),
    )(*flat)


def loss_head(xs, target, *, t_ctx, name):
    t = xs.shape[0]
    r = _Rows(t, t_ctx)

    def body(x_ref, t_ref, l_ref, dx_ref):
        i = pl.program_id(0)
        lat = (i >= r.ncb).astype(F32)
        diff = (x_ref[...] - t_ref[...]) * lat
        dx_ref[...] = diff * (1.0 / D)
        part = jnp.sum(jnp.sum(diff * diff, axis=1, keepdims=True), axis=0, keepdims=True) * (0.5 / D)
        _acc(l_ref, jnp.broadcast_to(part, (1, LANES)), r.first())

    return pl.pallas_call(
        body, name=name, grid=(r.nb,),
        in_specs=[r.rows(D), pl.BlockSpec((r.rb, D), lambda i: (jnp.maximum(i - r.ncb, 0), 0))],
        out_specs=[r.const(1, LANES), r.rows(D)],
        out_shape=[jax.ShapeDtypeStruct((1, LANES), F32), jax.ShapeDtypeStruct((t, D), F32)],
        compiler_params=_cparams(("arbitrary",)),
    )(xs, target)


def _ada(cg, w):
    return dot_hi_plain(_silu(cg), w)


def ada_fwd(cg, ada_w, *, name):
    nl, _, n = ada_w.shape

    def body(c_ref, w_ref, o_ref):
        o_ref[...] = _ada(c_ref[...], w_ref[...])

    return pl.pallas_call(
        body, name=name, grid=(nl,),
        in_specs=[pl.BlockSpec((16, D), lambda l: (0, 0)), pl.BlockSpec((None, D, n), lambda l: (l, 0, 0))],
        out_specs=pl.BlockSpec((None, 16, n), lambda l: (l, 0, 0)),
        out_shape=jax.ShapeDtypeStruct((nl, 16, n), F32),
        compiler_params=_cparams(("arbitrary",)),
    )(cg, ada_w)


def ada_bwd(cg, ada_w, dmod, *, name):
    nl, _, n = ada_w.shape

    def body(c_ref, w_ref, g_ref, dw_ref, dc_ref):
        _, vjp = jax.vjp(_ada, c_ref[...], w_ref[...])
        dc, dw = vjp(g_ref[...])
        dw_ref[...] = dw
        _acc(dc_ref, dc, pl.program_id(0) == 0)

    return pl.pallas_call(
        body, name=name, grid=(nl,),
        in_specs=[pl.BlockSpec((16, D), lambda l: (0, 0)), pl.BlockSpec((None, D, n), lambda l: (l, 0, 0)),
                  pl.BlockSpec((None, 16, n), lambda l: (l, 0, 0))],
        out_specs=[pl.BlockSpec((None, D, n), lambda l: (l, 0, 0)), pl.BlockSpec((16, D), lambda l: (0, 0))],
        out_shape=[jax.ShapeDtypeStruct((nl, D, n), F32), jax.ShapeDtypeStruct((16, D), F32)],
        compiler_params=_cparams(("arbitrary",)),
    )(cg, ada_w, dmod)


def _row_block(rows):
    for br in range(512, 7, -8):
        if rows % br == 0:
            return br
    return rows


def adamw(w, g, m, v, *, name):
    rows, cols = w.shape
    br = _row_block(rows)

    def body(w_ref, g_ref, m_ref, v_ref, d_ref, nm_ref, nv_ref):
        gg = g_ref[...]
        nm = ADAM_B1 * m_ref[...] + (1.0 - ADAM_B1) * gg
        nv = ADAM_B2 * v_ref[...] + (1.0 - ADAM_B2) * (gg * gg)
        m_hat = nm / (1.0 - ADAM_B1 ** ADAM_STEP)
        v_hat = nv / (1.0 - ADAM_B2 ** ADAM_STEP)
        d_ref[...] = -ADAM_LR * (m_hat / (jnp.sqrt(v_hat) + ADAM_EPS) + ADAM_WD * w_ref[...])
        nm_ref[...] = nm
        nv_ref[...] = nv

    spec = pl.BlockSpec((br, cols), lambda i: (i, 0))
    sd = jax.ShapeDtypeStruct((rows, cols), F32)
    return pl.pallas_call(
        body, name=name, grid=(rows // br,), in_specs=[spec] * 4, out_specs=[spec] * 3, out_shape=[sd] * 3,
        compiler_params=_cparams(("parallel",)),
    )(w, g, m, v)


def sum_leading(a, out_dtype, *, name):
    n, rows, cols = a.shape
    br = _row_block(rows)

    def body(a_ref, o_ref):
        acc = a_ref[0].astype(F32)
        for i in range(1, n):
            acc = acc + a_ref[i].astype(F32)
        o_ref[...] = acc.astype(out_dtype)

    return pl.pallas_call(
        body, name=name, grid=(rows // br,),
        in_specs=[pl.BlockSpec((n, br, cols), lambda i: (0, i, 0))],
        out_specs=pl.BlockSpec((br, cols), lambda i: (i, 0)),
        out_shape=jax.ShapeDtypeStruct((rows, cols), out_dtype),
        compiler_params=_cparams(("parallel",)),
    )(a)


def _position():
    return lax.axis_index("x"), lax.axis_index("y"), lax.axis_index("c")


def _other_chips(x, y):
    return [(1 - x, y), (x, 1 - y), (1 - x, 1 - y)]


def all_gather8(blk, *, name):
    m_per, n = blk.shape

    def body(x_ref, out_ref, send_sems, recv_sems, local_sem):
        x, y, c = _position()
        me, sibling = (x, y, c), (x, y, 1 - c)
        chips = _other_chips(x, y)

        def rows(px, py, pc):
            return out_ref.at[pl.ds((4 * px + 2 * py + pc) * m_per, m_per), :]

        def copy(k, block, to, src=None):
            return pltpu.make_async_remote_copy(
                src_ref=rows(*block) if src is None else src, dst_ref=rows(*block),
                send_sem=send_sems.at[k], recv_sem=recv_sems.at[k], device_id=to, device_id_type=MESH)

        mine = pltpu.make_async_copy(x_ref, rows(*me), local_sem)
        mine.start()
        first = [copy(0, me, sibling, src=x_ref)]
        first += [copy(1 + j, me, (*chip, c), src=x_ref) for j, chip in enumerate(chips)]
        for cp in first:
            cp.start()
        passed = [copy(4 + j, (*chip, c), sibling) for j, chip in enumerate(chips)]
        for j, chip in enumerate(chips):
            copy(1 + j, (*chip, c), me).wait_recv()
            passed[j].start()
        copy(0, sibling, me).wait_recv()
        for j, chip in enumerate(chips):
            copy(4 + j, (*chip, 1 - c), me).wait_recv()
        for cp in first + passed:
            cp.wait_send()
        mine.wait()

    return pl.pallas_call(
        body, name=name,
        out_shape=jax.ShapeDtypeStruct((8 * m_per, n), blk.dtype),
        in_specs=[pl.BlockSpec(memory_space=pltpu.VMEM)],
        out_specs=pl.BlockSpec(memory_space=pltpu.VMEM),
        scratch_shapes=[pltpu.SemaphoreType.DMA((7,)), pltpu.SemaphoreType.DMA((7,)), pltpu.SemaphoreType.DMA],
        compiler_params=pltpu.CompilerParams(vmem_limit_bytes=VMEM_LIMIT),
    )(blk)


_ANY = pl.BlockSpec(memory_space=pl.ANY)


def pair_exchange(arrs, *, name):
    n = len(arrs)

    def body(*refs):
        ins, outs, send_sems, recv_sems = refs[:n], refs[n:2 * n], refs[2 * n], refs[2 * n + 1]
        x, y, c = _position()
        cps = [pltpu.make_async_remote_copy(src_ref=ins[k], dst_ref=outs[k], send_sem=send_sems.at[k],
                                            recv_sem=recv_sems.at[k], device_id=(x, y, 1 - c), device_id_type=MESH)
               for k in range(n)]
        for cp in cps:
            cp.start()
        for cp in cps:
            cp.wait()

    return pl.pallas_call(
        body, name=name, out_shape=[jax.ShapeDtypeStruct(a.shape, a.dtype) for a in arrs],
        in_specs=[_ANY] * n, out_specs=[_ANY] * n,
        scratch_shapes=[pltpu.SemaphoreType.DMA((n,)), pltpu.SemaphoreType.DMA((n,))],
    )(*arrs)


def chip_exchange(arrs, *, gather, name):
    n = len(arrs)

    def body(*refs):
        ins, outs = refs[:n], refs[n:2 * n]
        send_sems, recv_sems, local_sems = refs[2 * n:]
        x, y, c = _position()
        me = 2 * x + y
        chips = _other_chips(x, y)
        local = [pltpu.make_async_copy(ins[k] if gather else ins[k].at[me], outs[k].at[me], local_sems.at[k])
                 for k in range(n)]
        for cp in local:
            cp.start()
        cps = []
        for k in range(n):
            for j, (px, py) in enumerate(chips):
                peer = 2 * px + py
                cps.append(pltpu.make_async_remote_copy(
                    src_ref=ins[k] if gather else ins[k].at[peer], dst_ref=outs[k].at[me],
                    send_sem=send_sems.at[3 * k + j], recv_sem=recv_sems.at[3 * k + j],
                    device_id=(px, py, c), device_id_type=MESH))
        for cp in cps:
            cp.start()
        for k in range(n):
            for j, (px, py) in enumerate(chips):
                peer = 2 * px + py
                pltpu.make_async_remote_copy(
                    src_ref=ins[k] if gather else ins[k].at[peer], dst_ref=outs[k].at[peer],
                    send_sem=send_sems.at[3 * k + j], recv_sem=recv_sems.at[3 * k + j],
                    device_id=(px, py, c), device_id_type=MESH).wait_recv()
        for cp in cps:
            cp.wait_send()
        for cp in local:
            cp.wait()

    out_shape = [jax.ShapeDtypeStruct(((4,) + a.shape) if gather else a.shape, a.dtype) for a in arrs]
    return pl.pallas_call(
        body, name=name, out_shape=out_shape, in_specs=[_ANY] * n, out_specs=[_ANY] * n,
        scratch_shapes=[pltpu.SemaphoreType.DMA((3 * n,)), pltpu.SemaphoreType.DMA((3 * n,)),
                        pltpu.SemaphoreType.DMA((n,))],
    )(*arrs)


def sum_arrays(arrs, out_dtype, *, name):
    rows, cols = arrs[0].shape
    br = _row_block(rows)

    def body(*refs):
        acc = refs[0][...].astype(F32)
        for rf in refs[1:-1]:
            acc = acc + rf[...].astype(F32)
        refs[-1][...] = acc.astype(out_dtype)

    spec = pl.BlockSpec((br, cols), lambda i: (i, 0))
    return pl.pallas_call(
        body, name=name, grid=(rows // br,), in_specs=[spec] * len(arrs), out_specs=spec,
        out_shape=jax.ShapeDtypeStruct((rows, cols), out_dtype),
        compiler_params=_cparams(("parallel",)),
    )(*arrs)


_WEIGHTS = ('c_ctx', 'ada_w', 'ada_b', 'norm_mix_pre', 'norm_mix_post', 'norm_ffn_pre', 'norm_ffn_post', 'w_in',
            'w_out', 'gla_gate_up', 'gla_gate_b', 'gla_norm', 'ssd_conv_w', 'ssd_conv_b', 'ssd_dt_bias', 'ssd_a_log',
            'ssd_d', 'ssd_norm', 'ret_norm', 'ffn_w13', 'ffn_w2')
_BIG = ('ada_w', 'w_in', 'w_out', 'ffn_w13', 'ffn_w2')
_EXCHANGED = ('w_in', 'w_out', 'ffn_w13', 'ffn_w2')

GLA_CFG = ScanCfg("gla", _chunk_vector, GLA_HEADS, GLA_DV, LANES, 128, (None, None, (GLA_HEADS, GLA_DV), None))
SSD_CFG = ScanCfg("ssd", _chunk_shared, SSD_HEADS, SSD_DV, SSD_STATE, 128,
                  ((SSD_GROUPS, SSD_STATE), (SSD_GROUPS, SSD_STATE), (SSD_HEADS, SSD_DV), None, None))
RET_CFG = ScanCfg("ret", _chunk_const, RET_HEADS, RET_DH, RET_DH, 128, ((RET_HEADS, RET_DH),) * 3)


def _permute_cols(w):
    parts = [jnp.zeros((w.shape[0], n), w.dtype) if src is None else w[:, src:src + n] for src, n in _PERM]
    return jnp.concatenate(parts, axis=1)


def _unpermute_cols(dw):
    return jnp.concatenate([dw[:, s:s + n] for s, n in _UNPERM], axis=1)


def _rope_tables(t_ctx, t_lat):
    grid_w = 64
    rows = t_lat // grid_w
    row = np.repeat(np.arange(rows), grid_w).astype(np.float32)
    col = np.tile(np.arange(grid_w), rows).astype(np.float32)
    inv = (np.float32(10000.0) ** (-np.arange(16, dtype=np.float32) / np.float32(16))).astype(np.float32)
    ang = np.concatenate([row[:, None] * inv, col[:, None] * inv], axis=-1).astype(np.float32)
    cos, sin = np.cos(ang), np.sin(ang)
    cos_t = np.tile(np.concatenate([cos, cos], -1), (1, RET_HEADS))
    sin_t = np.tile(np.concatenate([-sin, sin], -1), (1, RET_HEADS))
    w = RET_HEADS * RET_DH
    cos_t = np.concatenate([np.ones((t_ctx, w)), cos_t], 0).astype(np.float32)
    sin_t = np.concatenate([np.zeros((t_ctx, w)), sin_t], 0).astype(np.float32)
    return jnp.asarray(cos_t), jnp.asarray(sin_t)


def _pad_lanes(v):
    v = v.reshape(1, -1)
    return jnp.pad(v, ((0, 0), (0, LANES - v.shape[1])))


def _pack(arrs, rows):
    flat = jnp.concatenate([a.reshape(-1) for a in arrs])
    return jnp.pad(flat, (0, rows * LANES - flat.shape[0])).reshape(rows, LANES)


def _unpack(packed, shapes):
    flat, out, i = packed.reshape(-1), [], 0
    for s in shapes:
        n = int(np.prod(s))
        out.append(flat[i:i + n].reshape(s))
        i += n
    return out


def _rows_for(shapes):
    n = sum(int(np.prod(s)) for s in shapes)
    return -(-n // (8 * LANES)) * 8


def _layer_params(a, l, conv_full):
    row = lambda v: v.reshape(1, -1)
    return dict(
        nmp=row(a['norm_mix_pre'][l]), nmpost=row(a['norm_mix_post'][l]), nfp=row(a['norm_ffn_pre'][l]),
        nfpost=row(a['norm_ffn_post'][l]),
        convw8=jnp.pad(conv_full[l], ((0, 3), (0, 0))), convb=row(a['ssd_conv_b'][l]),
        dtb=_pad_lanes(a['ssd_dt_bias'][l]), alog=_pad_lanes(a['ssd_a_log'][l]),
        up=a['gla_gate_up'][l], gbias=a['gla_gate_b'][l][:, None, :],
        gla_nw=row(a['gla_norm'][l]), dsk=row(jnp.repeat(a['ssd_d'][l], SSD_DV)), ssd_nw=row(a['ssd_norm'][l]),
        ret_nw=row(a['ret_norm'][l]))


def _layer_fwd(xs, mod, w, lp, consts, t_ctx, tag):
    cos, sin, pm = consts
    kw = dict(t_ctx=t_ctx)
    p, h1 = nm_matmul(xs, lp['nmp'], mod, w['w_in'], sh=0, sc=1, out_dtype=F32, name="in_proj", **kw)
    u = conv_fwd(p, lp['convw8'], lp['convb'], name="conv_fwd", **kw)
    act, sdt, sg = ssd_act_fwd(u, p, lp['dtb'], lp['alog'], name="ssd_act_fwd", **kw)
    qs, ggf, ggb = gla_pre_fwd(p, lp['up'], lp['gbias'], name="gla_pre_fwd", **kw)
    rq, rk = rope_pair([(p, 256, P_RQ // 256)], [(p, 256, P_RK // 256)], cos, sin, transpose=False,
                       name="rope_fwd", **kw)
    gla_in = lambda g: [_full(qs), (p, 128, P_GK // 128), (p, 256, P_GV // 256), _full(g)]
    ssd_in = [(act, 256, 3), (act, 256, 2), (act, 512, 0), _full(sg), _full(sdt)]
    ret_in = [_full(rq), _full(rk), (p, 256, P_RV // 256)]
    scans = dict(
        gla_f=(GLA_CFG, gla_in(ggf), False, {}), gla_b=(GLA_CFG, gla_in(ggb), True, {}),
        ssd_f=(SSD_CFG, ssd_in, False, dict(g_off=0)), ssd_b=(SSD_CFG, ssd_in, True, dict(g_off=SSD_HEADS)),
        ret_f=(RET_CFG, ret_in, False, {}), ret_b=(RET_CFG, ret_in, True, {}))
    so = {}
    for key, (cfg, ins, rev, extra) in scans.items():
        so[key] = scan_fwd(cfg, ins, t_ctx=t_ctx, reverse=rev, **extra)
    post_in = (so['gla_f'][0], so['gla_b'][0], so['ssd_f'][0], so['ssd_b'][0], so['ret_f'][0], so['ret_b'][0],
               p, act, lp['gla_nw'], lp['dsk'], lp['ssd_nw'], lp['ret_nw'], pm)
    y = mix_post_fwd(*post_in, name="mix_post_fwd", **kw)
    xs1, zmix = mm_postnorm(y, w['w_out'], xs, mod, lp['nmpost'], gt=2, swiglu=False, name="out_proj", **kw)
    u13, h2 = nm_matmul(xs1, lp['nfp'], mod, w['ffn_w13'], sh=3, sc=4, out_dtype=BF16, name="ffn_up", **kw)
    xs2, zffn, actf = mm_postnorm(u13, w['ffn_w2'], xs1, mod, lp['nfpost'], gt=5, swiglu=True, name="ffn_down", **kw)
    saved = dict(xs=xs, p=p, h1=h1, u=u, scans=scans, states={k: v[1] for k, v in so.items()}, post_in=post_in,
                 y=y, xs1=xs1, zmix=zmix, u13=u13, h2=h2, zffn=zffn, actf=actf)
    return xs2, saved


def _layer_bwd(dxs, sv, mod, w, lp, consts, t_ctx):
    cos, sin, pm = consts
    kw = dict(t_ctx=t_ctx)
    du13, dzb, dgt2, dnfpost = postnorm_bwd(dxs, sv['zffn'], mod, lp['nfpost'], w['ffn_w2'], sv['u13'], gt=5,
                                            name="ffn_down_bwd", **kw)
    dw2 = grad_matmul(sv['actf'], dzb, name="ffn_w2_grad")
    dxs1, dnfp, dsh2, dsc2 = nm_bwd(du13, w['ffn_w13'], sv['xs1'], lp['nfp'], mod, dxs, sh=3, sc=4,
                                    name="ffn_up_bwd", **kw)
    dw13 = grad_matmul(sv['h2'], du13, name="ffn_w13_grad")
    dy, dzb1, dgt1, dnmpost = postnorm_bwd(dxs1, sv['zmix'], mod, lp['nmpost'], w['w_out'], None, gt=2,
                                           name="out_proj_bwd", **kw)
    dwout = grad_matmul(sv['y'], dzb1, name="w_out_grad")
    (dgo, dso, dro, drg, dz, dgg, dxs_skip, dgla_nw, ddsk, dssd_nw, dret_nw) = mix_post_bwd(
        *sv['post_in'], dy, name="mix_post_bwd", **kw)
    douts = dict(gla=_full(dgo), ssd=_full(dso), ret=_full(dro))
    sb = {}
    for key, (cfg, ins, rev, extra) in sv['scans'].items():
        sb[key] = scan_bwd(cfg, ins, sv['states'][key], douts[key[:3]], t_ctx=t_ctx, reverse=rev, **extra)
    p = sv['p']
    drq, drk = rope_pair([_full(sb['ret_f'][0]), _full(sb['ret_b'][0])], [_full(sb['ret_f'][1]), _full(sb['ret_b'][1])],
                         cos, sin, transpose=True, name="rope_bwd", **kw)
    dgq, dlr, dup, dgbias = gla_pre_bwd(p, lp['up'], lp['gbias'], [sb['gla_f'][0], sb['gla_b'][0]],
                                        sb['gla_f'][3], sb['gla_b'][3], name="gla_pre_bwd", **kw)
    du, ddt, ddtb, dalog = ssd_act_bwd(
        sv['u'], p, lp['dtb'], lp['alog'], [sb['ssd_f'][2], sb['ssd_b'][2], dxs_skip], [sb['ssd_f'][1], sb['ssd_b'][1]],
        [sb['ssd_f'][0], sb['ssd_b'][0]], [sb['ssd_f'][4], sb['ssd_b'][4]], [sb['ssd_f'][3], sb['ssd_b'][3]],
        name="ssd_act_bwd", **kw)
    dxbc, dconvw8, dconvb = conv_bwd(p, du, lp['convw8'], name="conv_bwd", **kw)
    dp = dp_assemble([
        (P_XBC, 1024, [dxbc]), (P_RQ, 256, [drq]), (P_RK, 256, [drk]), (P_RV, 256, [sb['ret_f'][2], sb['ret_b'][2]]),
        (P_RG, 256, [dgg]), (P_Z, 512, [dz]), (P_GV, 256, [sb['gla_f'][2], sb['gla_b'][2]]), (P_GR, 256, [drg]),
        (P_GQ, 128, [dgq]), (P_GK, 128, [sb['gla_f'][1], sb['gla_b'][1]]), (P_LR, 128, [dlr]), (P_DT, 128, [ddt])],
        name="dp_assemble", **kw)
    dxs0, dnmp, dsh1, dsc1 = nm_bwd(dp, w['w_in'], sv['xs'], lp['nmp'], mod, dxs1, sh=0, sc=1, name="in_proj_bwd", **kw)
    dwin = grad_matmul(sv['h1'], dp, name="w_in_grad")
    dmod = jnp.concatenate([dsh1, dsc1, dgt1, dsh2, dsc2, dgt2], axis=1)
    small = dict(
        norm_mix_pre=dnmp, norm_mix_post=dnmpost, norm_ffn_pre=dnfp, norm_ffn_post=dnfpost,
        gla_gate_up=dup, gla_gate_b=dgbias[:, 0, :], gla_norm=dgla_nw,
        ssd_conv_w=dconvw8[0:5], ssd_conv_b=dconvb, ssd_dt_bias=ddtb[0, 0:16].reshape(2, SSD_HEADS),
        ssd_a_log=dalog[0, 0:16].reshape(2, SSD_HEADS), ssd_d=ddsk.reshape(SSD_HEADS, SSD_DV).sum(-1),
        ssd_norm=dssd_nw, ret_norm=dret_nw)
    big = dict(w_in=_unpermute_cols(dwin), w_out=dwout, ffn_w13=dw13, ffn_w2=dw2)
    return dxs0, big, dmod, small


def _take_chips(g, m_per):
    return g.reshape(8, m_per, g.shape[1])[0::2]


def kernel(x, c, ctx, c_ctx, ada_w, ada_b, norm_mix_pre, norm_mix_post, norm_ffn_pre, norm_ffn_post, w_in, w_out, gla_gate_up, gla_gate_b, gla_norm, ssd_conv_w, ssd_conv_b, ssd_dt_bias, ssd_a_log, ssd_d, ssd_norm, ret_norm, ffn_w13, ffn_w2, loss_target, m_c_ctx, m_ada_w, m_ada_b, m_norm_mix_pre, m_norm_mix_post, m_norm_ffn_pre, m_norm_ffn_post, m_w_in, m_w_out, m_gla_gate_up, m_gla_gate_b, m_gla_norm, m_ssd_conv_w, m_ssd_conv_b, m_ssd_dt_bias, m_ssd_a_log, m_ssd_d, m_ssd_norm, m_ret_norm, m_ffn_w13, m_ffn_w2, v_c_ctx, v_ada_w, v_ada_b, v_norm_mix_pre, v_norm_mix_post, v_norm_ffn_pre, v_norm_ffn_post, v_w_in, v_w_out, v_gla_gate_up, v_gla_gate_b, v_gla_norm, v_ssd_conv_w, v_ssd_conv_b, v_ssd_dt_bias, v_ssd_a_log, v_ssd_d, v_ssd_norm, v_ret_norm, v_ffn_w13, v_ffn_w2):
    a = dict(locals())
    depth = ada_w.shape[0]
    t_ctx, t_lat = ctx.shape[1], x.shape[1]
    xi, yi, ci = _position()
    dev, chip = 4 * xi + 2 * yi + ci, 2 * xi + yi
    ncol = ada_w.shape[2]

    cw = ssd_conv_w.reshape(-1)
    blk = jnp.concatenate([c[0], cw, jnp.zeros((7 * D - cw.shape[0],), F32)]).reshape(8, D)
    g0 = all_gather8(blk, name="gather_cond").reshape(8, 8, D)
    conv_full = g0[0::2, 1:8].reshape(4, 7 * D)[:, :cw.shape[0]].reshape(4, depth, 5, D // 4)
    conv_full = conv_full.transpose(1, 2, 0, 3).reshape(depth, 5, D)
    cg = jnp.concatenate([g0[:, 0], jnp.broadcast_to(c_ctx[None], (8, D))], axis=0)
    part = ada_fwd(cg, ada_w, name="ada_fwd")
    g1 = _take_chips(all_gather8(part.reshape(depth * 16, ncol), name="gather_mod"), depth * 16)
    mod_all = g1.reshape(4, depth, 16, ncol).transpose(1, 2, 0, 3).reshape(depth, 16, 4 * ncol) + ada_b[:, None, :]
    mods = [jnp.stack([mod_all[l, 8], lax.dynamic_index_in_dim(mod_all[l], dev, 0, keepdims=False)]).reshape(2, 6, D)
            for l in range(depth)]

    halves = [lax.dynamic_slice_in_dim(a[n].astype(BF16), (depth // 2) * ci, depth // 2, axis=0) for n in _EXCHANGED]
    mine = chip_exchange(halves, gather=True, name="weight_gather")
    theirs = pair_exchange(mine, name="weight_pair")

    def layer_weights(l):
        hl, il = l // (depth // 2), l % (depth // 2)
        sh = {n: jnp.where(ci == hl, mine[k][:, il], theirs[k][:, il]) for k, n in enumerate(_EXCHANGED)}
        cols = lambda s: s.transpose(1, 0, 2).reshape(s.shape[1], 4 * s.shape[2])
        rows = lambda s: s.reshape(4 * s.shape[1], s.shape[2])
        return dict(w_in=_permute_cols(cols(sh['w_in'])), w_out=rows(sh['w_out']), ffn_w13=cols(sh['ffn_w13']),
                    ffn_w2=rows(sh['ffn_w2']))

    cos, sin = _rope_tables(t_ctx, t_lat)
    pm = np.kron(np.eye(RET_HEADS), np.full((RET_DH, RET_DH), 1.0 / RET_DH)).astype(np.float32)
    consts = (cos, sin, jnp.asarray(pm))

    xs = jnp.concatenate([ctx[0], x[0]], axis=0)
    saved, lws, lps = [], [], []
    for l in range(depth):
        lws.append(layer_weights(l))
        lps.append(_layer_params(a, l, conv_full))
        xs, sv = _layer_fwd(xs, mods[l], lws[l], lps[l], consts, t_ctx, l)
        saved.append(sv)
    lvec, dxs = loss_head(xs, loss_target[0], t_ctx=t_ctx, name="loss_head")
    loss = lax.psum(lvec[0, 0], ("x", "y", "c"))

    big = {n: [None] * depth for n in _EXCHANGED}
    small = [None] * depth
    dmods = [None] * depth
    for l in reversed(range(depth)):
        dxs, bg, dmods[l], small[l] = _layer_bwd(dxs, saved[l], mods[l], lws[l], lps[l], consts, t_ctx)
        for n in _EXCHANGED:
            big[n][l] = bg[n]
    grad_x = dxs[t_ctx:][None]

    def shard_major(n, g):
        g = jnp.stack(g)
        if n in ('w_in', 'ffn_w13'):
            return g.reshape(depth, g.shape[1], 4, g.shape[2] // 4).transpose(2, 0, 1, 3)
        return g.reshape(depth, 4, g.shape[1] // 4, g.shape[2]).transpose(1, 0, 2, 3)

    hd = depth // 2
    gsm = [shard_major(n, big[n]).astype(BF16) for n in _EXCHANGED]
    keep = [lax.dynamic_slice_in_dim(g, hd * ci, hd, axis=1) for g in gsm]
    give = [lax.dynamic_slice_in_dim(g, hd * (1 - ci), hd, axis=1) for g in gsm]
    got = pair_exchange(give, name="grad_pair")
    two = lambda g: g.reshape(-1, g.shape[-1])
    pair = [sum_arrays([two(k_), two(g_)], BF16, name="grad_pair_sum").reshape(k_.shape) for k_, g_ in zip(keep, got)]
    from_chips = chip_exchange(pair, gather=False, name="grad_scatter")
    mine_sum = [sum_leading(g.reshape(4, -1, g.shape[-1]), F32, name="grad_chip_sum") for g in from_chips]
    sib_sum = pair_exchange(mine_sum, name="grad_pair_back")
    grads = {}
    for k, n in enumerate(_EXCHANGED):
        both = jnp.where(ci == 0, jnp.concatenate([mine_sum[k], sib_sum[k]]), jnp.concatenate([sib_sum[k], mine_sum[k]]))
        grads[n] = both.reshape(a[n].shape)

    small_names = [n for n in _WEIGHTS if n not in _BIG and n not in ('c_ctx', 'ada_b')]
    small_shapes = [((depth, 5, D) if n == 'ssd_conv_w' else a[n].shape) for n in small_names]
    srows = _rows_for(small_shapes)
    dm = jnp.stack(dmods).reshape(depth, 2, 6 * D)
    mrows = dm.size // LANES
    vec = jnp.concatenate([_pack([jnp.stack([small[l][n] for l in range(depth)]) for n in small_names], srows),
                           dm.reshape(mrows, LANES)], axis=0)
    g2 = all_gather8(vec, name="gather_small").reshape(8, srows + mrows, LANES)
    small_sum = sum_leading(g2[:, :srows], F32, name="small_sum")
    sg = dict(zip(small_names, _unpack(small_sum, small_shapes)))
    dm_all = g2[:, srows:].reshape(8, depth, 2, 6 * D)
    grads['ada_b'] = sum_leading(dm_all.transpose(0, 2, 1, 3).reshape(16, depth * 6 * D // LANES, LANES), F32,
                                 name="ada_b_sum").reshape(depth, 6 * D)
    dm_cols = lax.dynamic_slice_in_dim(dm_all, chip * ncol, ncol, axis=3)
    dmod16 = jnp.concatenate([dm_cols[:, :, 1].transpose(1, 0, 2), dm_cols[:, :, 0].transpose(1, 0, 2)], axis=1)
    grads['ada_w'], dcg = ada_bwd(cg, ada_w, dmod16, name="ada_bwd")
    dcc = sum_leading(dcg[8:16].reshape(8, 1, D), F32, name="c_ctx_rows_sum")
    g3 = all_gather8(jnp.zeros((8, D), F32).at[0:1].set(dcc), name="gather_c_ctx").reshape(8, 8, D)
    grads['c_ctx'] = sum_leading(g3[0::2, 0:1], F32, name="c_ctx_sum").reshape(D)
    for n in small_names:
        grads[n] = sg[n]
    conv_grad_shard = lax.dynamic_slice_in_dim(sg['ssd_conv_w'], chip * (D // 4), D // 4, axis=2)
    grads['ssd_conv_w'] = conv_grad_shard

    delta, new_m, new_v = {}, {}, {}
    for n in _BIG:
        sh = a[n].shape
        two_d = lambda v: v.reshape(-1, sh[-1])
        d_, m_, v_ = adamw(two_d(a[n]), two_d(grads[n]), two_d(a['m_' + n]), two_d(a['v_' + n]), name="adamw_" + n)
        delta[n], new_m[n], new_v[n] = d_.reshape(sh), m_.reshape(sh), v_.reshape(sh)
    packed_names = [n for n in _WEIGHTS if n not in _BIG]
    shapes = [a[n].shape for n in packed_names]
    prow = _rows_for(shapes)
    pk = lambda pre: _pack([(grads[n] if pre == 'g' else a[pre + n]) for n in packed_names], prow)
    d_, m_, v_ = adamw(pk(''), pk('g'), pk('m_'), pk('v_'), name="adamw_small")
    for n, dd, mm, vv in zip(packed_names, _unpack(d_, shapes), _unpack(m_, shapes), _unpack(v_, shapes)):
        delta[n], new_m[n], new_v[n] = dd, mm, vv

    return (loss, grad_x, *[grads[n] for n in _WEIGHTS], *[delta[n] for n in _WEIGHTS],
            *[new_m[n] for n in _WEIGHTS], *[new_v[n] for n in _WEIGHTS])
```

```python
import functools
import math

import numpy as np
import jax
import jax.numpy as jnp
from jax import lax
from jax.experimental import pallas as pl
from jax.experimental.pallas import tpu as pltpu

F32 = jnp.float32
BF16 = jnp.bfloat16
MESH = pl.DeviceIdType.MESH

D = 1024
DEPTH = 4
RMS_EPS = 1e-6
GLA_HEADS, GLA_DK, GLA_DV = 4, 32, 64
SSD_HEADS, SSD_DV, SSD_STATE, SSD_GROUPS = 8, 64, 128, 2
RET_HEADS, RET_DH = 4, 64
FFN_HIDDEN = 2816
IN_COLS = 3376
ADAM_LR, ADAM_B1, ADAM_B2, ADAM_EPS, ADAM_WD, ADAM_STEP = 0.001, 0.9, 0.999, 1e-08, 0.01, 10

P_XBC, P_RQ, P_RK, P_RV, P_RG, P_Z, P_GV, P_GR, P_GQ, P_GK, P_LR, P_DT = (
    0, 1024, 1280, 1536, 1792, 2048, 2560, 2816, 3072, 3200, 3328, 3456)
NP = 3584
_PERM = ((1312, 1024), (2352, 1024), (800, 512), (256, 256), (512, 256), (0, 128), (128, 128), (768, 32),
         (None, 96), (2336, 16), (None, 112))
_UNPERM = ((3072, 128), (3200, 128), (2560, 256), (2816, 256), (3328, 32), (2048, 512), (0, 1024), (3456, 16),
           (1024, 1024))

LANES = 128
VMEM_LIMIT = 56 * 1024 * 1024
TN_CHUNK = 512


def _cparams(sem=None):
    kw = dict(vmem_limit_bytes=VMEM_LIMIT)
    if sem is not None:
        kw["dimension_semantics"] = sem
    return pltpu.CompilerParams(**kw)


def _dot(a, b, ca, cb):
    return lax.dot_general(a.astype(BF16), b.astype(BF16), (((ca,), (cb,)), ((), ())),
                           preferred_element_type=F32)


@jax.custom_vjp
def mm_nn(a, b):
    return _dot(a, b, 1, 0)


@jax.custom_vjp
def mm_nt(a, b):
    return _dot(a, b, 1, 1)


@jax.custom_vjp
def mm_tn(a, b):
    return _dot(a, b, 0, 0)


mm_nn.defvjp(lambda a, b: (mm_nn(a, b), (a, b)), lambda r, g: (mm_nt(g, r[1]), mm_tn(r[0], g)))
mm_nt.defvjp(lambda a, b: (mm_nt(a, b), (a, b)), lambda r, g: (mm_nn(g, r[1]), mm_tn(g, r[0])))
mm_tn.defvjp(lambda a, b: (mm_tn(a, b), (a, b)), lambda r, g: (mm_nt(r[1], g), mm_nn(r[0], g)))


def _hi(a, b, ca, cb):
    return lax.dot_general(a, b, (((ca,), (cb,)), ((), ())), precision=lax.Precision.HIGHEST,
                           preferred_element_type=F32)


def dot_hi_plain(a, b):
    return _hi(a, b, 1, 0)


@jax.custom_vjp
def dot_hi(a, b):
    return _hi(a, b, 1, 0)


dot_hi.defvjp(lambda a, b: (dot_hi(a, b), (a, b)), lambda r, g: (_hi(g, r[1], 1, 1), _hi(r[0].T, g, 1, 0)))


def _sigmoid(x):
    return 1.0 / (1.0 + jnp.exp(-x))


def _silu(x):
    return x * _sigmoid(x)


def _softplus(x):
    return jnp.maximum(x, 0.0) + jnp.log(1.0 + jnp.exp(-jnp.abs(x)))


def _log_sigmoid(x):
    return -_softplus(-x)


def _order_mask(n, reverse):
    r = lax.broadcasted_iota(jnp.int32, (n, n), 0)
    c = lax.broadcasted_iota(jnp.int32, (n, n), 1)
    return ((c >= r) if reverse else (c <= r)).astype(F32)


RET_LOG_GAMMA = tuple(math.log1p(-(2.0 ** (-5.0 - h))) for h in range(RET_HEADS))


def _chunk_vector(q, k, vs, g, sts, *, reverse):
    mask = _order_mask(q.shape[0], reverse)
    cum = dot_hi(mask, g)
    tot = jnp.sum(g, axis=0, keepdims=True)
    mid = 0.5 * tot
    qt = q * jnp.exp(jnp.minimum(cum - mid, 80.0))
    kt = k * jnp.exp(jnp.minimum(mid - cum, 80.0))
    qe = q * jnp.exp(cum)
    ke = k * jnp.exp(tot - cum)
    dec = jnp.exp(tot)
    lane = lax.broadcasted_iota(jnp.int32, (1, LANES), 1)
    outs, new = [], []
    for h in range(GLA_HEADS):
        hm = ((lane >= h * GLA_DK) & (lane < (h + 1) * GLA_DK)).astype(F32)
        a = mm_nt(qt * hm, kt) * mask
        outs.append(mm_nn(a, vs[h]) + mm_nt(qe * hm, sts[h]))
        new.append(sts[h] * dec + mm_tn(vs[h], ke * hm))
    return outs, new


def _chunk_shared(cms, bms, vs, g, dt, sts, *, reverse, g_off):
    n = g.shape[0]
    mask = _order_mask(n, reverse)
    cum = dot_hi(mask, g)
    tot = jnp.sum(g, axis=0, keepdims=True)
    cum_t, dt_t = cum.T, dt.T
    lane = lax.broadcasted_iota(jnp.int32, (1, LANES), 1)
    sub = lax.broadcasted_iota(jnp.int32, (LANES, 1), 0)
    rep = SSD_HEADS // SSD_GROUPS
    cb = [mm_nt(cms[i], bms[i]) for i in range(SSD_GROUPS)]
    outs, new = [], []
    for h in range(SSD_HEADS):
        pl_, ps_ = (lane == g_off + h).astype(F32), (sub == g_off + h).astype(F32)
        gh = jnp.sum(cum * pl_, axis=1, keepdims=True)
        th = jnp.sum(tot * pl_, axis=1, keepdims=True)
        dt_col = jnp.sum(dt * pl_, axis=1, keepdims=True)
        g_row = jnp.sum(cum_t * ps_, axis=0, keepdims=True)
        dt_row = jnp.sum(dt_t * ps_, axis=0, keepdims=True)
        a = cb[h // rep] * (jnp.exp(jnp.minimum(gh - g_row, 0.0)) * mask * dt_row)
        outs.append(mm_nn(a, vs[h]) + mm_nt(cms[h // rep] * jnp.exp(gh), sts[h]))
        new.append(sts[h] * jnp.exp(th) + mm_tn(vs[h] * (dt_col * jnp.exp(th - gh)), bms[h // rep]))
    return outs, new


def _chunk_const(qs, ks, vs, sts, *, reverse):
    n = qs[0].shape[0]
    r = lax.broadcasted_iota(jnp.int32, (n, n), 0)
    c = lax.broadcasted_iota(jnp.int32, (n, n), 1)
    dist = ((c - r) if reverse else (r - c)).astype(F32)
    row = lax.broadcasted_iota(jnp.int32, (n, 1), 0).astype(F32)
    seen = (n - row) if reverse else (row + 1.0)
    outs, new = [], []
    for h in range(RET_HEADS):
        gm = RET_LOG_GAMMA[h]
        dec = jnp.where(dist >= 0.0, jnp.exp(gm * dist), 0.0)
        a = mm_nt(qs[h], ks[h]) * dec
        outs.append(mm_nn(a, vs[h]) + mm_nt(qs[h] * jnp.exp(gm * seen), sts[h]))
        new.append(sts[h] * math.exp(gm * n) + mm_tn(vs[h], ks[h] * jnp.exp(gm * (n - seen))))
    return outs, new


class ScanCfg:
    def __init__(self, name, fn, heads, dv, st_k, chunk, parts):
        self.name, self.fn, self.heads, self.dv, self.st_k, self.chunk, self.parts = name, fn, heads, dv, st_k, chunk, parts

    def width(self, i):
        return LANES if self.parts[i] is None else self.parts[i][0] * self.parts[i][1]


def _scan_load(cfg, refs):
    vals = []
    for rf, part in zip(refs, cfg.parts):
        if part is None:
            vals.append(rf[...].astype(F32))
        else:
            vals.append([rf[:, i * part[1]:(i + 1) * part[1]].astype(F32) for i in range(part[0])])
    return vals


def _scan_store(cfg, refs, grads):
    for rf, part, g in zip(refs, cfg.parts, grads):
        if part is None:
            rf[...] = g
        else:
            for i in range(part[0]):
                rf[:, i * part[1]:(i + 1) * part[1]] = g[i]


def _chunk_of_step(s, n, nc, reverse):
    if not reverse:
        return s
    return jnp.where(s < nc, nc - 1 - s, n + nc - 1 - s)


def _arr_spec(a, c, pick):
    arr, w, cb = a
    return pl.BlockSpec((c, w), lambda s: (pick(s), cb))


def scan_fwd(cfg, ins, *, t_ctx, reverse, **kw):
    t = ins[0][0].shape[0]
    c, h, ni = cfg.chunk, cfg.heads, len(ins)
    n, nc = t // c, t_ctx // c
    pick = lambda s: _chunk_of_step(s, n, nc, reverse)

    def body(*refs):
        o_ref, st_ref, state = refs[ni:]

        @pl.when(pl.program_id(0) == 0)
        def _():
            state[...] = jnp.zeros_like(state)

        st_ref[...] = state[...]
        outs, new = cfg.fn(*_scan_load(cfg, refs[:ni]), [state[i] for i in range(h)], reverse=reverse, **kw)
        for i in range(h):
            o_ref[:, i * cfg.dv:(i + 1) * cfg.dv] = outs[i]
            state[i] = new[i]

    return pl.pallas_call(
        body, name=f"scan_fwd_{cfg.name}_{'b' if reverse else 'f'}", grid=(n,),
        in_specs=[_arr_spec(a, c, pick) for a in ins],
        out_specs=[pl.BlockSpec((c, h * cfg.dv), lambda s: (pick(s), 0)),
                   pl.BlockSpec((None, h, cfg.dv, cfg.st_k), lambda s: (pick(s), 0, 0, 0))],
        out_shape=[jax.ShapeDtypeStruct((t, h * cfg.dv), F32),
                   jax.ShapeDtypeStruct((n, h, cfg.dv, cfg.st_k), F32)],
        scratch_shapes=[pltpu.VMEM((h, cfg.dv, cfg.st_k), F32)],
        compiler_params=_cparams(("arbitrary",)),
    )(*[a[0] for a in ins])


def scan_bwd(cfg, ins, states, do, *, t_ctx, reverse, **kw):
    t = ins[0][0].shape[0]
    c, h, ni = cfg.chunk, cfg.heads, len(ins)
    n, nc = t // c, t_ctx // c
    pick = lambda s: _chunk_of_step(n - 1 - s, n, nc, reverse)

    def body(*refs):
        st_ref, do_ref = refs[ni:ni + 2]
        grad_refs, dstate = refs[ni + 2:-1], refs[-1]

        @pl.when(pl.program_id(0) == 0)
        def _():
            dstate[...] = jnp.zeros_like(dstate)

        dos = [do_ref[:, i * cfg.dv:(i + 1) * cfg.dv] for i in range(h)]
        _, vjp = jax.vjp(functools.partial(cfg.fn, reverse=reverse, **kw), *_scan_load(cfg, refs[:ni]),
                         [st_ref[i] for i in range(h)])
        grads = vjp((dos, [dstate[i] for i in range(h)]))
        _scan_store(cfg, grad_refs, grads[:-1])
        for i in range(h):
            dstate[i] = grads[-1][i]

    row = lambda w: pl.BlockSpec((c, w), lambda s: (pick(s), 0))
    return pl.pallas_call(
        body, name=f"scan_bwd_{cfg.name}_{'b' if reverse else 'f'}", grid=(n,),
        in_specs=[_arr_spec(a, c, pick) for a in ins]
        + [pl.BlockSpec((None, h, cfg.dv, cfg.st_k), lambda s: (pick(s), 0, 0, 0)), _arr_spec(do, c, pick)],
        out_specs=[row(cfg.width(i)) for i in range(ni)],
        out_shape=[jax.ShapeDtypeStruct((t, cfg.width(i)), F32) for i in range(ni)],
        scratch_shapes=[pltpu.VMEM((h, cfg.dv, cfg.st_k), F32)],
        compiler_params=_cparams(("arbitrary",)),
    )(*[a[0] for a in ins], states, do[0])


def _rb(t_ctx):
    return min(256, t_ctx)


class _Rows:
    def __init__(self, t, t_ctx):
        self.rb = _rb(t_ctx)
        self.nb, self.ncb = t // self.rb, t_ctx // self.rb

    def seg(self, i):
        return jnp.where(i >= self.ncb, 1, 0)

    def rows(self, w, cb=0):
        return pl.BlockSpec((self.rb, w), lambda i: (i, cb))

    def arr(self, a):
        return self.rows(a[1], a[2])

    def const(self, *shape):
        return pl.BlockSpec(shape, lambda i: (0,) * len(shape))

    def per_seg(self, *shape):
        return pl.BlockSpec((None,) + shape, lambda i: (self.seg(i),) + (0,) * len(shape))

    def first(self):
        return pl.program_id(0) == 0

    def seg_first(self):
        i = pl.program_id(0)
        return (i == 0) | (i == self.ncb)


def _acc(ref, val, first):
    @pl.when(first)
    def _():
        ref[...] = val

    @pl.when(jnp.logical_not(first))
    def _():
        ref[...] += val


def _full(a):
    return (a, a.shape[1], 0)


def _norm_mod(x, nw, sh, sc):
    y = x * lax.rsqrt(jnp.mean(x * x, axis=-1, keepdims=True) + RMS_EPS)
    return (y * nw) * (1.0 + sc) + sh


def _gated_norm(z, gt, pw):
    y = z * lax.rsqrt(jnp.mean(z * z, axis=-1, keepdims=True) + RMS_EPS)
    return gt * (y * pw)


def _swiglu(g, u):
    return _silu(g) * u


def nm_matmul(x, nw, mod, w, *, sh, sc, t_ctx, out_dtype, name):
    t, n = x.shape[0], w.shape[1]
    r = _Rows(t, t_ctx)

    def body(x_ref, nw_ref, mod_ref, w_ref, o_ref, h_ref):
        h = _norm_mod(x_ref[...], nw_ref[...], mod_ref[sh:sh + 1, :], mod_ref[sc:sc + 1, :]).astype(BF16)
        h_ref[...] = h
        for j in range(n // TN_CHUNK):
            sl = slice(j * TN_CHUNK, (j + 1) * TN_CHUNK)
            o_ref[:, sl] = _dot(h, w_ref[:, sl], 1, 0).astype(out_dtype)

    return pl.pallas_call(
        body, name=name, grid=(r.nb,),
        in_specs=[r.rows(D), r.const(1, D), r.per_seg(6, D), r.const(D, n)],
        out_specs=[r.rows(n), r.rows(D)],
        out_shape=[jax.ShapeDtypeStruct((t, n), out_dtype), jax.ShapeDtypeStruct((t, D), BF16)],
        compiler_params=_cparams(("arbitrary",)),
    )(x, nw, mod, w)


def nm_bwd(dout, w, x, nw, mod, dres, *, sh, sc, t_ctx, name):
    t, n = x.shape[0], w.shape[1]
    r = _Rows(t, t_ctx)

    def body(do_ref, w_ref, x_ref, nw_ref, mod_ref, dres_ref, dx_ref, dnw_ref, dsh_ref, dsc_ref):
        dh = jnp.zeros((r.rb, D), F32)
        for j in range(n // TN_CHUNK):
            sl = slice(j * TN_CHUNK, (j + 1) * TN_CHUNK)
            dh = dh + _dot(do_ref[:, sl], w_ref[:, sl], 1, 1)
        _, vjp = jax.vjp(_norm_mod, x_ref[...], nw_ref[...], mod_ref[sh:sh + 1, :], mod_ref[sc:sc + 1, :])
        dx, dnw, dsh, dsc = vjp(dh)
        dx_ref[...] = dx + dres_ref[...]
        _acc(dnw_ref, dnw, r.first())
        _acc(dsh_ref, dsh, r.seg_first())
        _acc(dsc_ref, dsc, r.seg_first())

    return pl.pallas_call(
        body, name=name, grid=(r.nb,),
        in_specs=[r.rows(n), r.const(D, n), r.rows(D), r.const(1, D), r.per_seg(6, D), r.rows(D)],
        out_specs=[r.rows(D), r.const(1, D), r.per_seg(1, D), r.per_seg(1, D)],
        out_shape=[jax.ShapeDtypeStruct((t, D), F32), jax.ShapeDtypeStruct((1, D), F32),
                   jax.ShapeDtypeStruct((2, 1, D), F32), jax.ShapeDtypeStruct((2, 1, D), F32)],
        compiler_params=_cparams(("arbitrary",)),
    )(dout, w, x, nw, mod, dres)


def mm_postnorm(a, w, xres, mod, pw, *, gt, swiglu, t_ctx, name):
    t, k = xres.shape[0], w.shape[0]
    r = _Rows(t, t_ctx)

    def body(a_ref, w_ref, x_ref, mod_ref, pw_ref, xn_ref, z_ref, *act_ref):
        if swiglu:
            act = _swiglu(a_ref[:, :k].astype(F32), a_ref[:, k:].astype(F32)).astype(BF16)
            act_ref[0][...] = act
        else:
            act = a_ref[...]
        z = _dot(act, w_ref[...], 1, 0)
        z_ref[...] = z
        xn_ref[...] = x_ref[...] + _gated_norm(z, mod_ref[gt:gt + 1, :], pw_ref[...])

    outs = [jax.ShapeDtypeStruct((t, D), F32), jax.ShapeDtypeStruct((t, D), F32)]
    ospecs = [r.rows(D), r.rows(D)]
    if swiglu:
        outs.append(jax.ShapeDtypeStruct((t, k), BF16))
        ospecs.append(r.rows(k))
    return pl.pallas_call(
        body, name=name, grid=(r.nb,),
        in_specs=[r.rows(a.shape[1]), r.const(k, D), r.rows(D), r.per_seg(6, D), r.const(1, D)],
        out_specs=ospecs, out_shape=outs,
        compiler_params=_cparams(("arbitrary",)),
    )(a, w, xres, mod, pw)


def postnorm_bwd(dxn, z, mod, pw, w, u13, *, gt, t_ctx, name):
    t, k = z.shape[0], w.shape[0]
    r = _Rows(t, t_ctx)
    swiglu = u13 is not None

    def body(dxn_ref, z_ref, mod_ref, pw_ref, w_ref, *rest):
        if swiglu:
            u_ref, da_ref, dz_ref, dgt_ref, dpw_ref = rest
        else:
            da_ref, dz_ref, dgt_ref, dpw_ref = rest
        _, vjp = jax.vjp(_gated_norm, z_ref[...], mod_ref[gt:gt + 1, :], pw_ref[...])
        dz, dgt, dpw = vjp(dxn_ref[...])
        dzb = dz.astype(BF16)
        dz_ref[...] = dzb
        da = _dot(dzb, w_ref[...], 1, 1)
        if swiglu:
            _, vjp2 = jax.vjp(_swiglu, u_ref[:, :k].astype(F32), u_ref[:, k:].astype(F32))
            dg, du = vjp2(da)
            da_ref[:, :k] = dg.astype(BF16)
            da_ref[:, k:] = du.astype(BF16)
        else:
            da_ref[...] = da
        _acc(dgt_ref, dgt, r.seg_first())
        _acc(dpw_ref, dpw, r.first())

    ins = [dxn, z, mod, pw, w] + ([u13] if swiglu else [])
    in_specs = [r.rows(D), r.rows(D), r.per_seg(6, D), r.const(1, D), r.const(k, D)] + ([r.rows(2 * k)] if swiglu else [])
    da_shape = jax.ShapeDtypeStruct((t, 2 * k), BF16) if swiglu else jax.ShapeDtypeStruct((t, k), F32)
    return pl.pallas_call(
        body, name=name, grid=(r.nb,), in_specs=in_specs,
        out_specs=[r.rows(2 * k if swiglu else k), r.rows(D), r.per_seg(1, D), r.const(1, D)],
        out_shape=[da_shape, jax.ShapeDtypeStruct((t, D), BF16), jax.ShapeDtypeStruct((2, 1, D), F32),
                   jax.ShapeDtypeStruct((1, D), F32)],
        compiler_params=_cparams(("arbitrary",)),
    )(*ins)


def grad_matmul(a, b, *, name):
    t, k = a.shape
    n = b.shape[1]
    tt = t // 4
    tn = TN_CHUNK if n % TN_CHUNK == 0 else n
    tk = k if k <= 1024 else k // 2

    def body(a_ref, b_ref, o_ref):
        @pl.when(pl.program_id(2) == 0)
        def _():
            o_ref[...] = jnp.zeros_like(o_ref)

        o_ref[...] += _dot(a_ref[...], b_ref[...], 0, 0)

    return pl.pallas_call(
        body, name=name, grid=(k // tk, n // tn, t // tt),
        in_specs=[pl.BlockSpec((tt, tk), lambda i, j, s: (s, i)), pl.BlockSpec((tt, tn), lambda i, j, s: (s, j))],
        out_specs=pl.BlockSpec((tk, tn), lambda i, j, s: (i, j)),
        out_shape=jax.ShapeDtypeStruct((k, n), F32),
        compiler_params=_cparams(("parallel", "parallel", "arbitrary")),
    )(a, b)


def _shifted(cur, prev, nxt, d, has_prev, has_next):
    n = cur.shape[0]
    row = lax.broadcasted_iota(jnp.int32, cur.shape, 0)
    if d == 0:
        return cur
    if d < 0:
        return jnp.where(row < -d, pltpu.roll(prev, -d, 0) * has_prev, pltpu.roll(cur, -d, 0))
    return jnp.where(row >= n - d, pltpu.roll(nxt, n - d, 0) * has_next, pltpu.roll(cur, n - d, 0))


def _halo_specs(r, w, cb):
    return [pl.BlockSpec((r.rb, w), lambda i: (jnp.maximum(i - 1, 0), cb)),
            pl.BlockSpec((r.rb, w), lambda i: (i, cb)),
            pl.BlockSpec((r.rb, w), lambda i: (jnp.minimum(i + 1, r.nb - 1), cb))]


def _halo_flags(r):
    i = pl.program_id(0)
    has_prev = ((i != 0) & (i != r.ncb)).astype(F32)
    has_next = ((i != r.ncb - 1) & (i != r.nb - 1)).astype(F32)
    return has_prev, has_next


def conv_fwd(p, w8, b, *, t_ctx, name):
    t = p.shape[0]
    r = _Rows(t, t_ctx)
    cw = 1024

    def body(pp_ref, pc_ref, pn_ref, w_ref, b_ref, u_ref):
        hp, hn = _halo_flags(r)
        prev, cur, nxt = pp_ref[...], pc_ref[...], pn_ref[...]
        acc = jnp.broadcast_to(b_ref[...], cur.shape)
        for kk in range(5):
            acc = acc + w_ref[kk:kk + 1, :] * _shifted(cur, prev, nxt, kk - 2, hp, hn)
        u_ref[...] = acc

    return pl.pallas_call(
        body, name=name, grid=(r.nb,),
        in_specs=_halo_specs(r, cw, P_XBC // cw) + [r.const(8, cw), r.const(1, cw)],
        out_specs=r.rows(cw), out_shape=jax.ShapeDtypeStruct((t, cw), F32),
        compiler_params=_cparams(("arbitrary",)),
    )(p, p, p, w8, b)


def conv_bwd(p, du, w8, *, t_ctx, name):
    t = p.shape[0]
    r = _Rows(t, t_ctx)
    cw = 1024

    def body(pp_ref, pc_ref, pn_ref, dp_ref, dc_ref, dn_ref, w_ref, dx_ref, dw_ref, db_ref):
        hp, hn = _halo_flags(r)
        prev, cur, nxt = pp_ref[...], pc_ref[...], pn_ref[...]
        dprev, dcur, dnxt = dp_ref[...], dc_ref[...], dn_ref[...]
        acc = jnp.zeros_like(dcur)
        for kk in range(5):
            acc = acc + w_ref[kk:kk + 1, :] * _shifted(dcur, dprev, dnxt, 2 - kk, hp, hn)
        dx_ref[...] = acc
        first = r.first()
        for kk in range(5):
            _acc(dw_ref.at[kk:kk + 1, :], jnp.sum(_shifted(cur, prev, nxt, kk - 2, hp, hn) * dcur, axis=0, keepdims=True),
                 first)
        _acc(dw_ref.at[5:8, :], jnp.zeros((3, cw), F32), first)
        _acc(db_ref, jnp.sum(dcur, axis=0, keepdims=True), first)

    return pl.pallas_call(
        body, name=name, grid=(r.nb,),
        in_specs=_halo_specs(r, cw, P_XBC // cw) + _halo_specs(r, cw, 0) + [r.const(8, cw)],
        out_specs=[r.rows(cw), r.const(8, cw), r.const(1, cw)],
        out_shape=[jax.ShapeDtypeStruct((t, cw), F32), jax.ShapeDtypeStruct((8, cw), F32),
                   jax.ShapeDtypeStruct((1, cw), F32)],
        compiler_params=_cparams(("arbitrary",)),
    )(p, p, p, du, du, du, w8)


def _ssd_act(u_x, u_b0, u_b1, u_c0, u_c1, dtraw, dtb, alog):
    dt = _softplus(dtraw + dtb)
    return _silu(u_x), _silu(u_b0), _silu(u_b1), _silu(u_c0), _silu(u_c1), dt, dt * (-jnp.exp(alog))


def _ssd_act_inputs(u_ref, dt_ref, dtb_ref, alog_ref):
    return (u_ref[:, 0:512], u_ref[:, 512:640], u_ref[:, 640:768], u_ref[:, 768:896], u_ref[:, 896:1024],
            dt_ref[...], dtb_ref[...], alog_ref[...])


_SSD_ACT_OFFSETS = ((0, 512), (512, 128), (640, 128), (768, 128), (896, 128))


def ssd_act_fwd(u, p, dtb, alog, *, t_ctx, name):
    t = u.shape[0]
    r = _Rows(t, t_ctx)

    def body(u_ref, dt_ref, dtb_ref, alog_ref, act_ref, dto_ref, g_ref):
        *pieces, dt, g = _ssd_act(*_ssd_act_inputs(u_ref, dt_ref, dtb_ref, alog_ref))
        for (off, w), val in zip(_SSD_ACT_OFFSETS, pieces):
            act_ref[:, off:off + w] = val
        dto_ref[...] = dt
        g_ref[...] = g

    return pl.pallas_call(
        body, name=name, grid=(r.nb,),
        in_specs=[r.rows(1024), r.rows(LANES, P_DT // LANES), r.const(1, LANES), r.const(1, LANES)],
        out_specs=[r.rows(1024), r.rows(LANES), r.rows(LANES)],
        out_shape=[jax.ShapeDtypeStruct((t, 1024), F32), jax.ShapeDtypeStruct((t, LANES), F32),
                   jax.ShapeDtypeStruct((t, LANES), F32)],
        compiler_params=_cparams(("arbitrary",)),
    )(u, p, dtb, alog)


def ssd_act_bwd(u, p, dtb, alog, dxs_list, db_list, dc_list, ddt_list, dg_list, *, t_ctx, name):
    t = u.shape[0]
    r = _Rows(t, t_ctx)
    lists = (dxs_list, db_list, dc_list, ddt_list, dg_list)
    widths = (512, 256, 256, LANES, LANES)

    def body(*refs):
        u_ref, dt_ref, dtb_ref, alog_ref = refs[:4]
        i, sums = 4, []
        for lst in lists:
            sums.append(sum(rf[...] for rf in refs[i:i + len(lst)]))
            i += len(lst)
        du_ref, ddt_ref, ddtb_ref, dalog_ref = refs[i:]
        dxs, dbm, dcm, ddt_ct, dg_ct = sums
        _, vjp = jax.vjp(_ssd_act, *_ssd_act_inputs(u_ref, dt_ref, dtb_ref, alog_ref))
        *dpieces, ddt, ddtb, dalog = vjp((dxs, dbm[:, 0:128], dbm[:, 128:256], dcm[:, 0:128], dcm[:, 128:256],
                                          ddt_ct, dg_ct))
        for (off, w), val in zip(_SSD_ACT_OFFSETS, dpieces):
            du_ref[:, off:off + w] = val
        ddt_ref[...] = ddt
        _acc(ddtb_ref, ddtb, r.first())
        _acc(dalog_ref, dalog, r.first())

    ins = [u, p, dtb, alog] + [a for lst in lists for a in lst]
    in_specs = ([r.rows(1024), r.rows(LANES, P_DT // LANES), r.const(1, LANES), r.const(1, LANES)]
                + [r.rows(w) for lst, w in zip(lists, widths) for _ in lst])
    return pl.pallas_call(
        body, name=name, grid=(r.nb,), in_specs=in_specs,
        out_specs=[r.rows(1024), r.rows(LANES), r.const(1, LANES), r.const(1, LANES)],
        out_shape=[jax.ShapeDtypeStruct((t, 1024), F32), jax.ShapeDtypeStruct((t, LANES), F32),
                   jax.ShapeDtypeStruct((1, LANES), F32), jax.ShapeDtypeStruct((1, LANES), F32)],
        compiler_params=_cparams(("arbitrary",)),
    )(*ins)


GLA_RANK = 16


def _gla_pre(q, lr, ups_f, ups_b, bf, bb):
    lane = lax.broadcasted_iota(jnp.int32, (1, LANES), 1)
    zf, zb = bf, bb
    for i in range(GLA_RANK):
        zf = zf + jnp.sum(lr * (lane == i).astype(F32), axis=1, keepdims=True) * ups_f[i]
        zb = zb + jnp.sum(lr * (lane == GLA_RANK + i).astype(F32), axis=1, keepdims=True) * ups_b[i]
    return q * (GLA_DK ** -0.5), _log_sigmoid(zf) * (1.0 / 16.0), _log_sigmoid(zb) * (1.0 / 16.0)


def _gla_pre_inputs(q_ref, lr_ref, up_ref, b_ref):
    rows = lambda d: [up_ref[d, i:i + 1, :] for i in range(GLA_RANK)]
    return q_ref[...], lr_ref[...], rows(0), rows(1), b_ref[0], b_ref[1]


def _gla_pre_specs(r):
    return [r.rows(LANES, P_GQ // LANES), r.rows(LANES, P_LR // LANES), r.const(2, GLA_RANK, LANES),
            r.const(2, 1, LANES)]


def gla_pre_fwd(p, up, gbias, *, t_ctx, name):
    t = p.shape[0]
    r = _Rows(t, t_ctx)

    def body(q_ref, lr_ref, up_ref, b_ref, qs_ref, gf_ref, gb_ref):
        qs, gf, gb = _gla_pre(*_gla_pre_inputs(q_ref, lr_ref, up_ref, b_ref))
        qs_ref[...] = qs
        gf_ref[...] = gf
        gb_ref[...] = gb

    sd = jax.ShapeDtypeStruct((t, LANES), F32)
    return pl.pallas_call(
        body, name=name, grid=(r.nb,), in_specs=_gla_pre_specs(r),
        out_specs=[r.rows(LANES)] * 3, out_shape=[sd, sd, sd],
        compiler_params=_cparams(("arbitrary",)),
    )(p, p, up, gbias)


def gla_pre_bwd(p, up, gbias, dq_list, dgf, dgb, *, t_ctx, name):
    t = p.shape[0]
    r = _Rows(t, t_ctx)
    nq = len(dq_list)

    def body(*refs):
        q_ref, lr_ref, up_ref, b_ref = refs[:4]
        dq = sum(rf[...] for rf in refs[4:4 + nq])
        dgf_ref, dgb_ref, dqo_ref, dlr_ref, dup_ref, db_ref = refs[4 + nq:]
        _, vjp = jax.vjp(_gla_pre, *_gla_pre_inputs(q_ref, lr_ref, up_ref, b_ref))
        dqo, dlr, dups_f, dups_b, dbf, dbb = vjp((dq, dgf_ref[...], dgb_ref[...]))
        dqo_ref[...] = dqo
        dlr_ref[...] = dlr

        def write(add):
            for d, (dups, dbias) in enumerate(((dups_f, dbf), (dups_b, dbb))):
                for i in range(GLA_RANK):
                    dup_ref[d, i:i + 1, :] = dups[i] + (dup_ref[d, i:i + 1, :] if add else 0.0)
                db_ref[d] = dbias + (db_ref[d] if add else 0.0)

        first = r.first()
        pl.when(first)(lambda: write(False))
        pl.when(jnp.logical_not(first))(lambda: write(True))

    sd = jax.ShapeDtypeStruct((t, LANES), F32)
    return pl.pallas_call(
        body, name=name, grid=(r.nb,), in_specs=_gla_pre_specs(r) + [r.rows(LANES)] * (nq + 2),
        out_specs=[r.rows(LANES), r.rows(LANES), r.const(2, GLA_RANK, LANES), r.const(2, 1, LANES)],
        out_shape=[sd, sd, jax.ShapeDtypeStruct((2, GLA_RANK, LANES), F32), jax.ShapeDtypeStruct((2, 1, LANES), F32)],
        compiler_params=_cparams(("arbitrary",)),
    )(p, p, up, gbias, *dq_list, dgf, dgb)


def _swap_halves(v):
    lane = lax.broadcasted_iota(jnp.int32, v.shape, 1)
    w = v.shape[1]
    return jnp.where((lane & 63) >= 32, pltpu.roll(v, 32, 1), pltpu.roll(v, w - 32, 1))


def rope_pair(q_list, k_list, cos, sin, *, transpose, t_ctx, name):
    t = cos.shape[0]
    r = _Rows(t, t_ctx)
    nq, nk = len(q_list), len(k_list)
    w = RET_HEADS * RET_DH

    def body(*refs):
        q = sum(rf[...] for rf in refs[:nq])
        k = sum(rf[...] for rf in refs[nq:nq + nk])
        cos_ref, sin_ref, qo_ref, ko_ref = refs[nq + nk:]
        cs, sn = cos_ref[...], sin_ref[...]
        if transpose:
            rot = lambda v: v * cs + _swap_halves(v * sn)
        else:
            rot = lambda v: v * cs + _swap_halves(v) * sn
        qo_ref[...] = rot(q) * (RET_DH ** -0.5)
        ko_ref[...] = rot(k)

    sd = jax.ShapeDtypeStruct((t, w), F32)
    return pl.pallas_call(
        body, name=name, grid=(r.nb,),
        in_specs=[r.arr(a) for a in list(q_list) + list(k_list)] + [r.rows(w), r.rows(w)],
        out_specs=[r.rows(w), r.rows(w)], out_shape=[sd, sd],
        compiler_params=_cparams(("arbitrary",)),
    )(*[a[0] for a in list(q_list) + list(k_list)], cos, sin)


def _gla_post(o_f, o_b, r_gate, nw, pm):
    o = o_f + o_b
    ms = dot_hi(o * o, pm)
    return o * lax.rsqrt(ms + RMS_EPS) * nw * _silu(r_gate)


def _ssd_post(o_f, o_b, xs, z, dsk, nw):
    y = (o_f + o_b + dsk * xs) * _silu(z)
    return y * lax.rsqrt(jnp.mean(y * y, axis=-1, keepdims=True) + RMS_EPS) * nw


def _ret_post(o_f, o_b, gate, nw, pm):
    o = o_f + o_b
    xc = o - dot_hi(o, pm)
    var = dot_hi(xc * xc, pm)
    return xc * lax.rsqrt(var + RMS_EPS) * nw * _silu(gate)


def _mix_post_specs(r):
    return [r.rows(256), r.rows(256), r.rows(512), r.rows(512), r.rows(256), r.rows(256),
            r.rows(256, P_GR // 256), r.rows(512, P_Z // 512), r.rows(256, P_RG // 256),
            r.rows(512, 0),
            r.const(1, 256), r.const(1, 512), r.const(1, 512), r.const(1, 256), r.const(256, 256)]


def mix_post_fwd(go_f, go_b, so_f, so_b, ro_f, ro_b, p, act, gla_nw, dsk, ssd_nw, ret_nw, pm, *, t_ctx, name):
    t = p.shape[0]
    r = _Rows(t, t_ctx)

    def body(gf, gb, sf, sb, rf, rbk, rg, z, gg, xs, gnw, dk, snw, rnw, pmr, y_ref):
        y_ref[:, 0:256] = _gla_post(gf[...], gb[...], rg[...], gnw[...], pmr[...]).astype(BF16)
        y_ref[:, 256:768] = _ssd_post(sf[...], sb[...], xs[...], z[...], dk[...], snw[...]).astype(BF16)
        y_ref[:, 768:1024] = _ret_post(rf[...], rbk[...], gg[...], rnw[...], pmr[...]).astype(BF16)

    return pl.pallas_call(
        body, name=name, grid=(r.nb,), in_specs=_mix_post_specs(r),
        out_specs=r.rows(D), out_shape=jax.ShapeDtypeStruct((t, D), BF16),
        compiler_params=_cparams(("arbitrary",)),
    )(go_f, go_b, so_f, so_b, ro_f, ro_b, p, p, p, act, gla_nw, dsk, ssd_nw, ret_nw, pm)


def mix_post_bwd(go_f, go_b, so_f, so_b, ro_f, ro_b, p, act, gla_nw, dsk, ssd_nw, ret_nw, pm, dy, *, t_ctx, name):
    t = p.shape[0]
    r = _Rows(t, t_ctx)

    def body(gf, gb, sf, sb, rf, rbk, rg, z, gg, xs, gnw, dk, snw, rnw, pmr, dy_ref,
             dgo, dso, dro, drg, dz, dgg, dxs, dgnw, ddk, dsnw, drnw):
        first = r.first()
        pmv = pmr[...]
        _, vjp = jax.vjp(lambda a, b, c, d: _gla_post(a, b, c, d, pmv), gf[...], gb[...], rg[...], gnw[...])
        a, _, c, d = vjp(dy_ref[:, 0:256])
        dgo[...] = a
        drg[...] = c
        _acc(dgnw, d, first)
        _, vjp = jax.vjp(_ssd_post, sf[...], sb[...], xs[...], z[...], dk[...], snw[...])
        a, _, c, d, e, f = vjp(dy_ref[:, 256:768])
        dso[...] = a
        dxs[...] = c
        dz[...] = d
        _acc(ddk, e, first)
        _acc(dsnw, f, first)
        _, vjp = jax.vjp(lambda a, b, c, d: _ret_post(a, b, c, d, pmv), rf[...], rbk[...], gg[...], rnw[...])
        a, _, c, d = vjp(dy_ref[:, 768:1024])
        dro[...] = a
        dgg[...] = c
        _acc(drnw, d, first)

    sd = lambda w: jax.ShapeDtypeStruct((t, w), F32)
    sp = lambda w: jax.ShapeDtypeStruct((1, w), F32)
    return pl.pallas_call(
        body, name=name, grid=(r.nb,), in_specs=_mix_post_specs(r) + [r.rows(D)],
        out_specs=[r.rows(256), r.rows(512), r.rows(256), r.rows(256), r.rows(512), r.rows(256), r.rows(512),
                   r.const(1, 256), r.const(1, 512), r.const(1, 512), r.const(1, 256)],
        out_shape=[sd(256), sd(512), sd(256), sd(256), sd(512), sd(256), sd(512), sp(256), sp(512), sp(512), sp(256)],
        compiler_params=_cparams(("arbitrary",)),
    )(go_f, go_b, so_f, so_b, ro_f, ro_b, p, p, p, act, gla_nw, dsk, ssd_nw, ret_nw, pm, dy)


def dp_assemble(pieces, *, t_ctx, name):
    t = pieces[0][2][0].shape[0]
    r = _Rows(t, t_ctx)
    flat = [a for _, _, arrs in pieces for a in arrs]

    def body(*refs):
        o_ref = refs[-1]
        i = 0
        for start, w, arrs in pieces:
            tot = sum(refs[i + j][...] for j in range(len(arrs)))
            i += len(arrs)
            o_ref[:, start:start + w] = tot.astype(BF16)

    return pl.pallas_call(
        body, name=name, grid=(r.nb,),
        in_specs=[r.rows(w) for _, w, arrs in pieces for _ in arrs],
        out_specs=r.rows(NP), out_shape=jax.ShapeDtypeStruct((t, NP), BF16),
        compiler_params=_cparams(("arbitrary",)),
    )(*flat)


def loss_head(xs, target, *, t_ctx, name):
    t = xs.shape[0]
    r = _Rows(t, t_ctx)

    def body(x_ref, t_ref, l_ref, dx_ref):
        i = pl.program_id(0)
        lat = (i >= r.ncb).astype(F32)
        diff = (x_ref[...] - t_ref[...]) * lat
        dx_ref[...] = diff * (1.0 / D)
        part = jnp.sum(jnp.sum(diff * diff, axis=1, keepdims=True), axis=0, keepdims=True) * (0.5 / D)
        _acc(l_ref, jnp.broadcast_to(part, (1, LANES)), r.first())

    return pl.pallas_call(
        body, name=name, grid=(r.nb,),
        in_specs=[r.rows(D), pl.BlockSpec((r.rb, D), lambda i: (jnp.maximum(i - r.ncb, 0), 0))],
        out_specs=[r.const(1, LANES), r.rows(D)],
        out_shape=[jax.ShapeDtypeStruct((1, LANES), F32), jax.ShapeDtypeStruct((t, D), F32)],
        compiler_params=_cparams(("arbitrary",)),
    )(xs, target)


def _ada(cg, w):
    return dot_hi_plain(_silu(cg), w)


def ada_fwd(cg, ada_w, *, name):
    nl, _, n = ada_w.shape

    def body(c_ref, w_ref, o_ref):
        o_ref[...] = _ada(c_ref[...], w_ref[...])

    return pl.pallas_call(
        body, name=name, grid=(nl,),
        in_specs=[pl.BlockSpec((16, D), lambda l: (0, 0)), pl.BlockSpec((None, D, n), lambda l: (l, 0, 0))],
        out_specs=pl.BlockSpec((None, 16, n), lambda l: (l, 0, 0)),
        out_shape=jax.ShapeDtypeStruct((nl, 16, n), F32),
        compiler_params=_cparams(("arbitrary",)),
    )(cg, ada_w)


def ada_bwd(cg, ada_w, dmod, *, name):
    nl, _, n = ada_w.shape

    def body(c_ref, w_ref, g_ref, dw_ref, dc_ref):
        _, vjp = jax.vjp(_ada, c_ref[...], w_ref[...])
        dc, dw = vjp(g_ref[...])
        dw_ref[...] = dw
        _acc(dc_ref, dc, pl.program_id(0) == 0)

    return pl.pallas_call(
        body, name=name, grid=(nl,),
        in_specs=[pl.BlockSpec((16, D), lambda l: (0, 0)), pl.BlockSpec((None, D, n), lambda l: (l, 0, 0)),
                  pl.BlockSpec((None, 16, n), lambda l: (l, 0, 0))],
        out_specs=[pl.BlockSpec((None, D, n), lambda l: (l, 0, 0)), pl.BlockSpec((16, D), lambda l: (0, 0))],
        out_shape=[jax.ShapeDtypeStruct((nl, D, n), F32), jax.ShapeDtypeStruct((16, D), F32)],
        compiler_params=_cparams(("arbitrary",)),
    )(cg, ada_w, dmod)


def _row_block(rows):
    for br in range(512, 7, -8):
        if rows % br == 0:
            return br
    return rows


def adamw(w, g, m, v, *, name):
    rows, cols = w.shape
    br = _row_block(rows)

    def body(w_ref, g_ref, m_ref, v_ref, d_ref, nm_ref, nv_ref):
        gg = g_ref[...]
        nm = ADAM_B1 * m_ref[...] + (1.0 - ADAM_B1) * gg
        nv = ADAM_B2 * v_ref[...] + (1.0 - ADAM_B2) * (gg * gg)
        m_hat = nm / (1.0 - ADAM_B1 ** ADAM_STEP)
        v_hat = nv / (1.0 - ADAM_B2 ** ADAM_STEP)
        d_ref[...] = -ADAM_LR * (m_hat / (jnp.sqrt(v_hat) + ADAM_EPS) + ADAM_WD * w_ref[...])
        nm_ref[...] = nm
        nv_ref[...] = nv

    spec = pl.BlockSpec((br, cols), lambda i: (i, 0))
    sd = jax.ShapeDtypeStruct((rows, cols), F32)
    return pl.pallas_call(
        body, name=name, grid=(rows // br,), in_specs=[spec] * 4, out_specs=[spec] * 3, out_shape=[sd] * 3,
        compiler_params=_cparams(("parallel",)),
    )(w, g, m, v)


def sum_leading(a, out_dtype, *, name):
    n, rows, cols = a.shape
    br = _row_block(rows)

    def body(a_ref, o_ref):
        acc = a_ref[0].astype(F32)
        for i in range(1, n):
            acc = acc + a_ref[i].astype(F32)
        o_ref[...] = acc.astype(out_dtype)

    return pl.pallas_call(
        body, name=name, grid=(rows // br,),
        in_specs=[pl.BlockSpec((n, br, cols), lambda i: (0, i, 0))],
        out_specs=pl.BlockSpec((br, cols), lambda i: (i, 0)),
        out_shape=jax.ShapeDtypeStruct((rows, cols), out_dtype),
        compiler_params=_cparams(("parallel",)),
    )(a)


def _position():
    return lax.axis_index("x"), lax.axis_index("y"), lax.axis_index("c")


def _other_chips(x, y):
    return [(1 - x, y), (x, 1 - y), (1 - x, 1 - y)]


def all_gather8(blk, *, name):
    m_per, n = blk.shape

    def body(x_ref, out_ref, send_sems, recv_sems, local_sem):
        x, y, c = _position()
        me, sibling = (x, y, c), (x, y, 1 - c)
        chips = _other_chips(x, y)

        def rows(px, py, pc):
            return out_ref.at[pl.ds((4 * px + 2 * py + pc) * m_per, m_per), :]

        def copy(k, block, to, src=None):
            return pltpu.make_async_remote_copy(
                src_ref=rows(*block) if src is None else src, dst_ref=rows(*block),
                send_sem=send_sems.at[k], recv_sem=recv_sems.at[k], device_id=to, device_id_type=MESH)

        mine = pltpu.make_async_copy(x_ref, rows(*me), local_sem)
        mine.start()
        first = [copy(0, me, sibling, src=x_ref)]
        first += [copy(1 + j, me, (*chip, c), src=x_ref) for j, chip in enumerate(chips)]
        for cp in first:
            cp.start()
        passed = [copy(4 + j, (*chip, c), sibling) for j, chip in enumerate(chips)]
        for j, chip in enumerate(chips):
            copy(1 + j, (*chip, c), me).wait_recv()
            passed[j].start()
        copy(0, sibling, me).wait_recv()
        for j, chip in enumerate(chips):
            copy(4 + j, (*chip, 1 - c), me).wait_recv()
        for cp in first + passed:
            cp.wait_send()
        mine.wait()

    return pl.pallas_call(
        body, name=name,
        out_shape=jax.ShapeDtypeStruct((8 * m_per, n), blk.dtype),
        in_specs=[pl.BlockSpec(memory_space=pltpu.VMEM)],
        out_specs=pl.BlockSpec(memory_space=pltpu.VMEM),
        scratch_shapes=[pltpu.SemaphoreType.DMA((7,)), pltpu.SemaphoreType.DMA((7,)), pltpu.SemaphoreType.DMA],
        compiler_params=pltpu.CompilerParams(vmem_limit_bytes=VMEM_LIMIT),
    )(blk)


_ANY = pl.BlockSpec(memory_space=pl.ANY)


def pair_exchange(arrs, *, name):
    n = len(arrs)

    def body(*refs):
        ins, outs, send_sems, recv_sems = refs[:n], refs[n:2 * n], refs[2 * n], refs[2 * n + 1]
        x, y, c = _position()
        cps = [pltpu.make_async_remote_copy(src_ref=ins[k], dst_ref=outs[k], send_sem=send_sems.at[k],
                                            recv_sem=recv_sems.at[k], device_id=(x, y, 1 - c), device_id_type=MESH)
               for k in range(n)]
        for cp in cps:
            cp.start()
        for cp in cps:
            cp.wait()

    return pl.pallas_call(
        body, name=name, out_shape=[jax.ShapeDtypeStruct(a.shape, a.dtype) for a in arrs],
        in_specs=[_ANY] * n, out_specs=[_ANY] * n,
        scratch_shapes=[pltpu.SemaphoreType.DMA((n,)), pltpu.SemaphoreType.DMA((n,))],
    )(*arrs)


def chip_exchange(arrs, *, gather, name):
    n = len(arrs)

    def body(*refs):
        ins, outs = refs[:n], refs[n:2 * n]
        send_sems, recv_sems, local_sems = refs[2 * n:]
        x, y, c = _position()
        me = 2 * x + y
        chips = _other_chips(x, y)
        local = [pltpu.make_async_copy(ins[k] if gather else ins[k].at[me], outs[k].at[me], local_sems.at[k])
                 for k in range(n)]
        for cp in local:
            cp.start()
        cps = []
        for k in range(n):
            for j, (px, py) in enumerate(chips):
                peer = 2 * px + py
                cps.append(pltpu.make_async_remote_copy(
                    src_ref=ins[k] if gather else ins[k].at[peer], dst_ref=outs[k].at[me],
                    send_sem=send_sems.at[3 * k + j], recv_sem=recv_sems.at[3 * k + j],
                    device_id=(px, py, c), device_id_type=MESH))
        for cp in cps:
            cp.start()
        for k in range(n):
            for j, (px, py) in enumerate(chips):
                peer = 2 * px + py
                pltpu.make_async_remote_copy(
                    src_ref=ins[k] if gather else ins[k].at[peer], dst_ref=outs[k].at[peer],
                    send_sem=send_sems.at[3 * k + j], recv_sem=recv_sems.at[3 * k + j],
                    device_id=(px, py, c), device_id_type=MESH).wait_recv()
        for cp in cps:
            cp.wait_send()
        for cp in local:
            cp.wait()

    out_shape = [jax.ShapeDtypeStruct(((4,) + a.shape) if gather else a.shape, a.dtype) for a in arrs]
    return pl.pallas_call(
        body, name=name, out_shape=out_shape, in_specs=[_ANY] * n, out_specs=[_ANY] * n,
        scratch_shapes=[pltpu.SemaphoreType.DMA((3 * n,)), pltpu.SemaphoreType.DMA((3 * n,)),
                        pltpu.SemaphoreType.DMA((n,))],
    )(*arrs)


_HBM = pl.BlockSpec(memory_space=pltpu.HBM)
_SEM = pl.BlockSpec(memory_space=pltpu.SEMAPHORE)
_DATAFLOW = pltpu.SideEffectType.DATAFLOW_SIDE_EFFECTING


def _gather_copies(srcs, lands, send_sems, recv_sems, as_receiver):
    x, y, c = _position()
    me = 2 * x + y
    cps = []
    for k in range(len(srcs)):
        for j, (px, py) in enumerate(_other_chips(x, y)):
            slot = (2 * px + py) if as_receiver else me
            cps.append(pltpu.make_async_remote_copy(
                src_ref=srcs[k], dst_ref=lands[k].at[slot], send_sem=send_sems.at[3 * k + j],
                recv_sem=recv_sems.at[3 * k + j], device_id=(px, py, c), device_id_type=MESH))
    return cps


def gather_start(srcs, after, *, name):
    n = len(srcs)

    def body(*refs):
        ins, lands = refs[:n], refs[n:2 * n]
        send_sems, recv_sems = refs[2 * n + 1:2 * n + 3]
        token = refs[-1]
        for cp in _gather_copies(ins, lands, send_sems, recv_sems, False):
            cp.start()
        token[...] = jnp.zeros_like(token)

    hbm = lambda a: pltpu.with_memory_space_constraint(a, pltpu.HBM)
    zones = [lax.empty((4,) + s.shape, s.dtype) for s in srcs]
    outs = pl.pallas_call(
        body, name=name,
        out_shape=(pltpu.SemaphoreType.DMA((3 * n,)), pltpu.SemaphoreType.DMA((3 * n,)),
                   *[pltpu.HBM(s.shape, s.dtype) for s in srcs], *[pltpu.HBM(z.shape, z.dtype) for z in zones],
                   jax.ShapeDtypeStruct((8, LANES), F32)),
        in_specs=[_HBM] * (2 * n) + [pl.BlockSpec(memory_space=pl.ANY)],
        out_specs=(_SEM, _SEM, *[_HBM] * (2 * n), pl.BlockSpec(memory_space=pltpu.VMEM)),
        input_output_aliases={i: 2 + i for i in range(2 * n)},
        compiler_params=pltpu.CompilerParams(has_side_effects=_DATAFLOW),
    )(*[hbm(s) for s in srcs], *[hbm(z) for z in zones], after)
    return outs[0], outs[1], list(outs[2:2 + n]), list(outs[2 + n:2 + 2 * n]), outs[-1]


def gather_wait(send_sems, recv_sems, srcs, lands, after, *, name):
    n = len(srcs)

    def body(*refs):
        ins, zones = refs[:n], refs[n:2 * n]
        s_sems, r_sems = refs[2 * n:2 * n + 2]
        for cp in _gather_copies(ins, zones, s_sems, r_sems, True):
            cp.wait_send()
            cp.wait_recv()

    outs = pl.pallas_call(
        body, name=name,
        out_shape=(*[pltpu.HBM(s.shape, s.dtype) for s in srcs], *[pltpu.HBM(z.shape, z.dtype) for z in lands]),
        in_specs=[_HBM] * (2 * n) + [_SEM, _SEM, pl.BlockSpec(memory_space=pl.ANY)],
        out_specs=tuple([_HBM] * (2 * n)),
        input_output_aliases={i: i for i in range(2 * n)},
        compiler_params=pltpu.CompilerParams(has_side_effects=_DATAFLOW),
    )(*srcs, *lands, send_sems, recv_sems, after)
    return list(outs[n:])


def sum_arrays(arrs, out_dtype, *, name):
    rows, cols = arrs[0].shape
    br = _row_block(rows)

    def body(*refs):
        acc = refs[0][...].astype(F32)
        for rf in refs[1:-1]:
            acc = acc + rf[...].astype(F32)
        refs[-1][...] = acc.astype(out_dtype)

    spec = pl.BlockSpec((br, cols), lambda i: (i, 0))
    return pl.pallas_call(
        body, name=name, grid=(rows // br,), in_specs=[spec] * len(arrs), out_specs=spec,
        out_shape=jax.ShapeDtypeStruct((rows, cols), out_dtype),
        compiler_params=_cparams(("parallel",)),
    )(*arrs)


_WEIGHTS = ('c_ctx', 'ada_w', 'ada_b', 'norm_mix_pre', 'norm_mix_post', 'norm_ffn_pre', 'norm_ffn_post', 'w_in',
            'w_out', 'gla_gate_up', 'gla_gate_b', 'gla_norm', 'ssd_conv_w', 'ssd_conv_b', 'ssd_dt_bias', 'ssd_a_log',
            'ssd_d', 'ssd_norm', 'ret_norm', 'ffn_w13', 'ffn_w2')
_BIG = ('ada_w', 'w_in', 'w_out', 'ffn_w13', 'ffn_w2')
_EXCHANGED = ('w_in', 'w_out', 'ffn_w13', 'ffn_w2')

GLA_CFG = ScanCfg("gla", _chunk_vector, GLA_HEADS, GLA_DV, LANES, 128, (None, None, (GLA_HEADS, GLA_DV), None))
SSD_CFG = ScanCfg("ssd", _chunk_shared, SSD_HEADS, SSD_DV, SSD_STATE, 128,
                  ((SSD_GROUPS, SSD_STATE), (SSD_GROUPS, SSD_STATE), (SSD_HEADS, SSD_DV), None, None))
RET_CFG = ScanCfg("ret", _chunk_const, RET_HEADS, RET_DH, RET_DH, 128, ((RET_HEADS, RET_DH),) * 3)


def _permute_cols(w):
    parts = [jnp.zeros((w.shape[0], n), w.dtype) if src is None else w[:, src:src + n] for src, n in _PERM]
    return jnp.concatenate(parts, axis=1)


def _unpermute_cols(dw):
    return jnp.concatenate([dw[:, s:s + n] for s, n in _UNPERM], axis=1)


def _rope_tables(t_ctx, t_lat):
    grid_w = 64
    rows = t_lat // grid_w
    row = np.repeat(np.arange(rows), grid_w).astype(np.float32)
    col = np.tile(np.arange(grid_w), rows).astype(np.float32)
    inv = (np.float32(10000.0) ** (-np.arange(16, dtype=np.float32) / np.float32(16))).astype(np.float32)
    ang = np.concatenate([row[:, None] * inv, col[:, None] * inv], axis=-1).astype(np.float32)
    cos, sin = np.cos(ang), np.sin(ang)
    cos_t = np.tile(np.concatenate([cos, cos], -1), (1, RET_HEADS))
    sin_t = np.tile(np.concatenate([-sin, sin], -1), (1, RET_HEADS))
    w = RET_HEADS * RET_DH
    cos_t = np.concatenate([np.ones((t_ctx, w)), cos_t], 0).astype(np.float32)
    sin_t = np.concatenate([np.zeros((t_ctx, w)), sin_t], 0).astype(np.float32)
    return jnp.asarray(cos_t), jnp.asarray(sin_t)


def _pad_lanes(v):
    v = v.reshape(1, -1)
    return jnp.pad(v, ((0, 0), (0, LANES - v.shape[1])))


def _pack(arrs, rows):
    flat = jnp.concatenate([a.reshape(-1) for a in arrs])
    return jnp.pad(flat, (0, rows * LANES - flat.shape[0])).reshape(rows, LANES)


def _unpack(packed, shapes):
    flat, out, i = packed.reshape(-1), [], 0
    for s in shapes:
        n = int(np.prod(s))
        out.append(flat[i:i + n].reshape(s))
        i += n
    return out


def _rows_for(shapes):
    n = sum(int(np.prod(s)) for s in shapes)
    return -(-n // (8 * LANES)) * 8


def _layer_params(a, l, conv_full):
    row = lambda v: v.reshape(1, -1)
    return dict(
        nmp=row(a['norm_mix_pre'][l]), nmpost=row(a['norm_mix_post'][l]), nfp=row(a['norm_ffn_pre'][l]),
        nfpost=row(a['norm_ffn_post'][l]),
        convw8=jnp.pad(conv_full[l], ((0, 3), (0, 0))), convb=row(a['ssd_conv_b'][l]),
        dtb=_pad_lanes(a['ssd_dt_bias'][l]), alog=_pad_lanes(a['ssd_a_log'][l]),
        up=a['gla_gate_up'][l], gbias=a['gla_gate_b'][l][:, None, :],
        gla_nw=row(a['gla_norm'][l]), dsk=row(jnp.repeat(a['ssd_d'][l], SSD_DV)), ssd_nw=row(a['ssd_norm'][l]),
        ret_nw=row(a['ret_norm'][l]))


def _layer_fwd(xs, mod, w, lp, consts, t_ctx, tag):
    cos, sin, pm = consts
    kw = dict(t_ctx=t_ctx)
    p, h1 = nm_matmul(xs, lp['nmp'], mod, w['w_in'], sh=0, sc=1, out_dtype=F32, name="in_proj", **kw)
    u = conv_fwd(p, lp['convw8'], lp['convb'], name="conv_fwd", **kw)
    act, sdt, sg = ssd_act_fwd(u, p, lp['dtb'], lp['alog'], name="ssd_act_fwd", **kw)
    qs, ggf, ggb = gla_pre_fwd(p, lp['up'], lp['gbias'], name="gla_pre_fwd", **kw)
    rq, rk = rope_pair([(p, 256, P_RQ // 256)], [(p, 256, P_RK // 256)], cos, sin, transpose=False,
                       name="rope_fwd", **kw)
    gla_in = lambda g: [_full(qs), (p, 128, P_GK // 128), (p, 256, P_GV // 256), _full(g)]
    ssd_in = [(act, 256, 3), (act, 256, 2), (act, 512, 0), _full(sg), _full(sdt)]
    ret_in = [_full(rq), _full(rk), (p, 256, P_RV // 256)]
    scans = dict(
        gla_f=(GLA_CFG, gla_in(ggf), False, {}), gla_b=(GLA_CFG, gla_in(ggb), True, {}),
        ssd_f=(SSD_CFG, ssd_in, False, dict(g_off=0)), ssd_b=(SSD_CFG, ssd_in, True, dict(g_off=SSD_HEADS)),
        ret_f=(RET_CFG, ret_in, False, {}), ret_b=(RET_CFG, ret_in, True, {}))
    so = {}
    for key, (cfg, ins, rev, extra) in scans.items():
        so[key] = scan_fwd(cfg, ins, t_ctx=t_ctx, reverse=rev, **extra)
    post_in = (so['gla_f'][0], so['gla_b'][0], so['ssd_f'][0], so['ssd_b'][0], so['ret_f'][0], so['ret_b'][0],
               p, act, lp['gla_nw'], lp['dsk'], lp['ssd_nw'], lp['ret_nw'], pm)
    y = mix_post_fwd(*post_in, name="mix_post_fwd", **kw)
    xs1, zmix = mm_postnorm(y, w['w_out'], xs, mod, lp['nmpost'], gt=2, swiglu=False, name="out_proj", **kw)
    u13, h2 = nm_matmul(xs1, lp['nfp'], mod, w['ffn_w13'], sh=3, sc=4, out_dtype=BF16, name="ffn_up", **kw)
    xs2, zffn, actf = mm_postnorm(u13, w['ffn_w2'], xs1, mod, lp['nfpost'], gt=5, swiglu=True, name="ffn_down", **kw)
    saved = dict(xs=xs, p=p, h1=h1, u=u, scans=scans, states={k: v[1] for k, v in so.items()}, post_in=post_in,
                 y=y, xs1=xs1, zmix=zmix, u13=u13, h2=h2, zffn=zffn, actf=actf)
    return xs2, saved


def _layer_bwd(dxs, sv, mod, w, lp, consts, t_ctx):
    cos, sin, pm = consts
    kw = dict(t_ctx=t_ctx)
    du13, dzb, dgt2, dnfpost = postnorm_bwd(dxs, sv['zffn'], mod, lp['nfpost'], w['ffn_w2'], sv['u13'], gt=5,
                                            name="ffn_down_bwd", **kw)
    dw2 = grad_matmul(sv['actf'], dzb, name="ffn_w2_grad")
    dxs1, dnfp, dsh2, dsc2 = nm_bwd(du13, w['ffn_w13'], sv['xs1'], lp['nfp'], mod, dxs, sh=3, sc=4,
                                    name="ffn_up_bwd", **kw)
    dw13 = grad_matmul(sv['h2'], du13, name="ffn_w13_grad")
    dy, dzb1, dgt1, dnmpost = postnorm_bwd(dxs1, sv['zmix'], mod, lp['nmpost'], w['w_out'], None, gt=2,
                                           name="out_proj_bwd", **kw)
    dwout = grad_matmul(sv['y'], dzb1, name="w_out_grad")
    (dgo, dso, dro, drg, dz, dgg, dxs_skip, dgla_nw, ddsk, dssd_nw, dret_nw) = mix_post_bwd(
        *sv['post_in'], dy, name="mix_post_bwd", **kw)
    douts = dict(gla=_full(dgo), ssd=_full(dso), ret=_full(dro))
    sb = {}
    for key, (cfg, ins, rev, extra) in sv['scans'].items():
        sb[key] = scan_bwd(cfg, ins, sv['states'][key], douts[key[:3]], t_ctx=t_ctx, reverse=rev, **extra)
    p = sv['p']
    drq, drk = rope_pair([_full(sb['ret_f'][0]), _full(sb['ret_b'][0])], [_full(sb['ret_f'][1]), _full(sb['ret_b'][1])],
                         cos, sin, transpose=True, name="rope_bwd", **kw)
    dgq, dlr, dup, dgbias = gla_pre_bwd(p, lp['up'], lp['gbias'], [sb['gla_f'][0], sb['gla_b'][0]],
                                        sb['gla_f'][3], sb['gla_b'][3], name="gla_pre_bwd", **kw)
    du, ddt, ddtb, dalog = ssd_act_bwd(
        sv['u'], p, lp['dtb'], lp['alog'], [sb['ssd_f'][2], sb['ssd_b'][2], dxs_skip], [sb['ssd_f'][1], sb['ssd_b'][1]],
        [sb['ssd_f'][0], sb['ssd_b'][0]], [sb['ssd_f'][4], sb['ssd_b'][4]], [sb['ssd_f'][3], sb['ssd_b'][3]],
        name="ssd_act_bwd", **kw)
    dxbc, dconvw8, dconvb = conv_bwd(p, du, lp['convw8'], name="conv_bwd", **kw)
    dp = dp_assemble([
        (P_XBC, 1024, [dxbc]), (P_RQ, 256, [drq]), (P_RK, 256, [drk]), (P_RV, 256, [sb['ret_f'][2], sb['ret_b'][2]]),
        (P_RG, 256, [dgg]), (P_Z, 512, [dz]), (P_GV, 256, [sb['gla_f'][2], sb['gla_b'][2]]), (P_GR, 256, [drg]),
        (P_GQ, 128, [dgq]), (P_GK, 128, [sb['gla_f'][1], sb['gla_b'][1]]), (P_LR, 128, [dlr]), (P_DT, 128, [ddt])],
        name="dp_assemble", **kw)
    dxs0, dnmp, dsh1, dsc1 = nm_bwd(dp, w['w_in'], sv['xs'], lp['nmp'], mod, dxs1, sh=0, sc=1, name="in_proj_bwd", **kw)
    dwin = grad_matmul(sv['h1'], dp, name="w_in_grad")
    dmod = jnp.concatenate([dsh1, dsc1, dgt1, dsh2, dsc2, dgt2], axis=1)
    small = dict(
        norm_mix_pre=dnmp, norm_mix_post=dnmpost, norm_ffn_pre=dnfp, norm_ffn_post=dnfpost,
        gla_gate_up=dup, gla_gate_b=dgbias[:, 0, :], gla_norm=dgla_nw,
        ssd_conv_w=dconvw8[0:5], ssd_conv_b=dconvb, ssd_dt_bias=ddtb[0, 0:16].reshape(2, SSD_HEADS),
        ssd_a_log=dalog[0, 0:16].reshape(2, SSD_HEADS), ssd_d=ddsk.reshape(SSD_HEADS, SSD_DV).sum(-1),
        ssd_norm=dssd_nw, ret_norm=dret_nw)
    big = dict(w_in=_unpermute_cols(dwin), w_out=dwout, ffn_w13=dw13, ffn_w2=dw2)
    return dxs0, big, dmod, small


def _take_chips(g, m_per):
    return g.reshape(8, m_per, g.shape[1])[0::2]


def kernel(x, c, ctx, c_ctx, ada_w, ada_b, norm_mix_pre, norm_mix_post, norm_ffn_pre, norm_ffn_post, w_in, w_out, gla_gate_up, gla_gate_b, gla_norm, ssd_conv_w, ssd_conv_b, ssd_dt_bias, ssd_a_log, ssd_d, ssd_norm, ret_norm, ffn_w13, ffn_w2, loss_target, m_c_ctx, m_ada_w, m_ada_b, m_norm_mix_pre, m_norm_mix_post, m_norm_ffn_pre, m_norm_ffn_post, m_w_in, m_w_out, m_gla_gate_up, m_gla_gate_b, m_gla_norm, m_ssd_conv_w, m_ssd_conv_b, m_ssd_dt_bias, m_ssd_a_log, m_ssd_d, m_ssd_norm, m_ret_norm, m_ffn_w13, m_ffn_w2, v_c_ctx, v_ada_w, v_ada_b, v_norm_mix_pre, v_norm_mix_post, v_norm_ffn_pre, v_norm_ffn_post, v_w_in, v_w_out, v_gla_gate_up, v_gla_gate_b, v_gla_norm, v_ssd_conv_w, v_ssd_conv_b, v_ssd_dt_bias, v_ssd_a_log, v_ssd_d, v_ssd_norm, v_ret_norm, v_ffn_w13, v_ffn_w2):
    a = dict(locals())
    depth = ada_w.shape[0]
    t_ctx, t_lat = ctx.shape[1], x.shape[1]
    xi, yi, ci = _position()
    dev, chip = 4 * xi + 2 * yi + ci, 2 * xi + yi
    ncol = ada_w.shape[2]

    cw = ssd_conv_w.reshape(-1)
    blk = jnp.concatenate([c[0], cw, jnp.zeros((7 * D - cw.shape[0],), F32)]).reshape(8, D)
    g0 = all_gather8(blk, name="gather_cond").reshape(8, 8, D)
    conv_full = g0[0::2, 1:8].reshape(4, 7 * D)[:, :cw.shape[0]].reshape(4, depth, 5, D // 4)
    conv_full = conv_full.transpose(1, 2, 0, 3).reshape(depth, 5, D)
    cg = jnp.concatenate([g0[:, 0], jnp.broadcast_to(c_ctx[None], (8, D))], axis=0)
    part = ada_fwd(cg, ada_w, name="ada_fwd")
    g1 = _take_chips(all_gather8(part.reshape(depth * 16, ncol), name="gather_mod"), depth * 16)
    mod_all = g1.reshape(4, depth, 16, ncol).transpose(1, 2, 0, 3).reshape(depth, 16, 4 * ncol) + ada_b[:, None, :]
    mods = [jnp.stack([mod_all[l, 8], lax.dynamic_index_in_dim(mod_all[l], dev, 0, keepdims=False)]).reshape(2, 6, D)
            for l in range(depth)]

    def start_gather(l, after):
        halves = [lax.dynamic_slice_in_dim(a[n][l], (a[n].shape[1] // 2) * ci, a[n].shape[1] // 2, axis=0).astype(BF16)
                  for n in _EXCHANGED]
        return gather_start(halves, after, name="weight_gather_start")

    def finish_gather(started, after):
        send_sems, recv_sems, halves, zones, _ = started
        zones = gather_wait(send_sems, recv_sems, halves, zones, after, name="weight_gather_wait")
        mine = [lax.dynamic_update_slice_in_dim(z, h[None], chip, axis=0) for z, h in zip(zones, halves)]
        theirs = pair_exchange(mine, name="weight_pair")
        sh = {n: jnp.concatenate([jnp.where(ci == 0, mine[k], theirs[k]), jnp.where(ci == 0, theirs[k], mine[k])], axis=1)
              for k, n in enumerate(_EXCHANGED)}
        cols = lambda s: s.transpose(1, 0, 2).reshape(s.shape[1], 4 * s.shape[2])
        rows = lambda s: s.reshape(4 * s.shape[1], s.shape[2])
        return dict(w_in=_permute_cols(cols(sh['w_in'])), w_out=rows(sh['w_out']), ffn_w13=cols(sh['ffn_w13']),
                    ffn_w2=rows(sh['ffn_w2']))

    cos, sin = _rope_tables(t_ctx, t_lat)
    pm = np.kron(np.eye(RET_HEADS), np.full((RET_DH, RET_DH), 1.0 / RET_DH)).astype(np.float32)
    consts = (cos, sin, jnp.asarray(pm))

    xs = jnp.concatenate([ctx[0], x[0]], axis=0)
    saved, lws, lps = [], [], []
    started = start_gather(0, mod_all)
    for l in range(depth):
        lws.append(finish_gather(started, xs))
        mod = mods[l]
        if l + 1 < depth:
            started = start_gather(l + 1, lws[l]['w_out'])
            mod = mod + started[4][0, 0]
        lps.append(_layer_params(a, l, conv_full))
        xs, sv = _layer_fwd(xs, mod, lws[l], lps[l], consts, t_ctx, l)
        saved.append(sv)
    lvec, dxs = loss_head(xs, loss_target[0], t_ctx=t_ctx, name="loss_head")
    loss = lax.psum(lvec[0, 0], ("x", "y", "c"))

    big = {n: [None] * depth for n in _EXCHANGED}
    small = [None] * depth
    dmods = [None] * depth
    for l in reversed(range(depth)):
        dxs, bg, dmods[l], small[l] = _layer_bwd(dxs, saved[l], mods[l], lws[l], lps[l], consts, t_ctx)
        for n in _EXCHANGED:
            big[n][l] = bg[n]
    grad_x = dxs[t_ctx:][None]

    def shard_major(n, g):
        g = jnp.stack(g)
        if n in ('w_in', 'ffn_w13'):
            return g.reshape(depth, g.shape[1], 4, g.shape[2] // 4).transpose(2, 0, 1, 3)
        return g.reshape(depth, 4, g.shape[1] // 4, g.shape[2]).transpose(1, 0, 2, 3)

    hd = depth // 2
    gsm = [shard_major(n, big[n]).astype(BF16) for n in _EXCHANGED]
    keep = [lax.dynamic_slice_in_dim(g, hd * ci, hd, axis=1) for g in gsm]
    give = [lax.dynamic_slice_in_dim(g, hd * (1 - ci), hd, axis=1) for g in gsm]
    got = pair_exchange(give, name="grad_pair")
    two = lambda g: g.reshape(-1, g.shape[-1])
    pair = [sum_arrays([two(k_), two(g_)], BF16, name="grad_pair_sum").reshape(k_.shape) for k_, g_ in zip(keep, got)]
    from_chips = chip_exchange(pair, gather=False, name="grad_scatter")
    mine_sum = [sum_leading(g.reshape(4, -1, g.shape[-1]), F32, name="grad_chip_sum") for g in from_chips]
    sib_sum = pair_exchange(mine_sum, name="grad_pair_back")
    grads = {}
    for k, n in enumerate(_EXCHANGED):
        both = jnp.where(ci == 0, jnp.concatenate([mine_sum[k], sib_sum[k]]), jnp.concatenate([sib_sum[k], mine_sum[k]]))
        grads[n] = both.reshape(a[n].shape)

    small_names = [n for n in _WEIGHTS if n not in _BIG and n not in ('c_ctx', 'ada_b')]
    small_shapes = [((depth, 5, D) if n == 'ssd_conv_w' else a[n].shape) for n in small_names]
    srows = _rows_for(small_shapes)
    dm = jnp.stack(dmods).reshape(depth, 2, 6 * D)
    mrows = dm.size // LANES
    vec = jnp.concatenate([_pack([jnp.stack([small[l][n] for l in range(depth)]) for n in small_names], srows),
                           dm.reshape(mrows, LANES)], axis=0)
    g2 = all_gather8(vec, name="gather_small").reshape(8, srows + mrows, LANES)
    small_sum = sum_leading(g2[:, :srows], F32, name="small_sum")
    sg = dict(zip(small_names, _unpack(small_sum, small_shapes)))
    dm_all = g2[:, srows:].reshape(8, depth, 2, 6 * D)
    grads['ada_b'] = sum_leading(dm_all.transpose(0, 2, 1, 3).reshape(16, depth * 6 * D // LANES, LANES), F32,
                                 name="ada_b_sum").reshape(depth, 6 * D)
    dm_cols = lax.dynamic_slice_in_dim(dm_all, chip * ncol, ncol, axis=3)
    dmod16 = jnp.concatenate([dm_cols[:, :, 1].transpose(1, 0, 2), dm_cols[:, :, 0].transpose(1, 0, 2)], axis=1)
    grads['ada_w'], dcg = ada_bwd(cg, ada_w, dmod16, name="ada_bwd")
    dcc = sum_leading(dcg[8:16].reshape(8, 1, D), F32, name="c_ctx_rows_sum")
    g3 = all_gather8(jnp.zeros((8, D), F32).at[0:1].set(dcc), name="gather_c_ctx").reshape(8, 8, D)
    grads['c_ctx'] = sum_leading(g3[0::2, 0:1], F32, name="c_ctx_sum").reshape(D)
    for n in small_names:
        grads[n] = sg[n]
    conv_grad_shard = lax.dynamic_slice_in_dim(sg['ssd_conv_w'], chip * (D // 4), D // 4, axis=2)
    grads['ssd_conv_w'] = conv_grad_shard

    delta, new_m, new_v = {}, {}, {}
    for n in _BIG:
        sh = a[n].shape
        two_d = lambda v: v.reshape(-1, sh[-1])
        d_, m_, v_ = adamw(two_d(a[n]), two_d(grads[n]), two_d(a['m_' + n]), two_d(a['v_' + n]), name="adamw_" + n)
        delta[n], new_m[n], new_v[n] = d_.reshape(sh), m_.reshape(sh), v_.reshape(sh)
    packed_names = [n for n in _WEIGHTS if n not in _BIG]
    shapes = [a[n].shape for n in packed_names]
    prow = _rows_for(shapes)
    pk = lambda pre: _pack([(grads[n] if pre == 'g' else a[pre + n]) for n in packed_names], prow)
    d_, m_, v_ = adamw(pk(''), pk('g'), pk('m_'), pk('v_'), name="adamw_small")
    for n, dd, mm, vv in zip(packed_names, _unpack(d_, shapes), _unpack(m_, shapes), _unpack(v_, shapes)):
        delta[n], new_m[n], new_v[n] = dd, mm, vv

    return (loss, grad_x, *[grads[n] for n in _WEIGHTS], *[delta[n] for n in _WEIGHTS],
            *[new_m[n] for n in _WEIGHTS], *[new_v[n] for n in _WEIGHTS])
```

```python
import functools
import math

import numpy as np
import jax
import jax.numpy as jnp
from jax import lax
from jax.experimental import pallas as pl
from jax.experimental.pallas import tpu as pltpu

F32 = jnp.float32
BF16 = jnp.bfloat16
MESH = pl.DeviceIdType.MESH

D = 1024
DEPTH = 4
RMS_EPS = 1e-6
GLA_HEADS, GLA_DK, GLA_DV = 4, 32, 64
SSD_HEADS, SSD_DV, SSD_STATE, SSD_GROUPS = 8, 64, 128, 2
RET_HEADS, RET_DH = 4, 64
FFN_HIDDEN = 2816
IN_COLS = 3376
ADAM_LR, ADAM_B1, ADAM_B2, ADAM_EPS, ADAM_WD, ADAM_STEP = 0.001, 0.9, 0.999, 1e-08, 0.01, 10

P_XBC, P_RQ, P_RK, P_RV, P_RG, P_Z, P_GV, P_GR, P_GQ, P_GK, P_LR, P_DT = (
    0, 1024, 1280, 1536, 1792, 2048, 2560, 2816, 3072, 3200, 3328, 3456)
NP = 3584
_PERM = ((1312, 1024), (2352, 1024), (800, 512), (256, 256), (512, 256), (0, 128), (128, 128), (768, 32),
         (None, 96), (2336, 16), (None, 112))
_UNPERM = ((3072, 128), (3200, 128), (2560, 256), (2816, 256), (3328, 32), (2048, 512), (0, 1024), (3456, 16),
           (1024, 1024))

LANES = 128
VMEM_LIMIT = 56 * 1024 * 1024
TN_CHUNK = 512


def _cparams(sem=None):
    kw = dict(vmem_limit_bytes=VMEM_LIMIT)
    if sem is not None:
        kw["dimension_semantics"] = sem
    return pltpu.CompilerParams(**kw)


def _dot(a, b, ca, cb):
    return lax.dot_general(a.astype(BF16), b.astype(BF16), (((ca,), (cb,)), ((), ())),
                           preferred_element_type=F32)


@jax.custom_vjp
def mm_nn(a, b):
    return _dot(a, b, 1, 0)


@jax.custom_vjp
def mm_nt(a, b):
    return _dot(a, b, 1, 1)


@jax.custom_vjp
def mm_tn(a, b):
    return _dot(a, b, 0, 0)


mm_nn.defvjp(lambda a, b: (mm_nn(a, b), (a, b)), lambda r, g: (mm_nt(g, r[1]), mm_tn(r[0], g)))
mm_nt.defvjp(lambda a, b: (mm_nt(a, b), (a, b)), lambda r, g: (mm_nn(g, r[1]), mm_tn(g, r[0])))
mm_tn.defvjp(lambda a, b: (mm_tn(a, b), (a, b)), lambda r, g: (mm_nt(r[1], g), mm_nn(r[0], g)))


def _hi(a, b, ca, cb):
    return lax.dot_general(a, b, (((ca,), (cb,)), ((), ())), precision=lax.Precision.HIGHEST,
                           preferred_element_type=F32)


def dot_hi_plain(a, b):
    return _hi(a, b, 1, 0)


@jax.custom_vjp
def dot_hi(a, b):
    return _hi(a, b, 1, 0)


dot_hi.defvjp(lambda a, b: (dot_hi(a, b), (a, b)), lambda r, g: (_hi(g, r[1], 1, 1), _hi(r[0].T, g, 1, 0)))


def _sigmoid(x):
    return 1.0 / (1.0 + jnp.exp(-x))


def _silu(x):
    return x * _sigmoid(x)


def _softplus(x):
    return jnp.maximum(x, 0.0) + jnp.log(1.0 + jnp.exp(-jnp.abs(x)))


def _log_sigmoid(x):
    return -_softplus(-x)


def _order_mask(n, reverse):
    r = lax.broadcasted_iota(jnp.int32, (n, n), 0)
    c = lax.broadcasted_iota(jnp.int32, (n, n), 1)
    return ((c >= r) if reverse else (c <= r)).astype(F32)


RET_LOG_GAMMA = tuple(math.log1p(-(2.0 ** (-5.0 - h))) for h in range(RET_HEADS))


def _chunk_vector(q, k, vs, g, sts, *, reverse):
    mask = _order_mask(q.shape[0], reverse)
    cum = dot_hi(mask, g)
    tot = jnp.sum(g, axis=0, keepdims=True)
    mid = 0.5 * tot
    qt = q * jnp.exp(jnp.minimum(cum - mid, 80.0))
    kt = k * jnp.exp(jnp.minimum(mid - cum, 80.0))
    qe = q * jnp.exp(cum)
    ke = k * jnp.exp(tot - cum)
    dec = jnp.exp(tot)
    lane = lax.broadcasted_iota(jnp.int32, (1, LANES), 1)
    outs, new = [], []
    for h in range(GLA_HEADS):
        hm = ((lane >= h * GLA_DK) & (lane < (h + 1) * GLA_DK)).astype(F32)
        a = mm_nt(qt * hm, kt) * mask
        outs.append(mm_nn(a, vs[h]) + mm_nt(qe * hm, sts[h]))
        new.append(sts[h] * dec + mm_tn(vs[h], ke * hm))
    return outs, new


def _chunk_shared(cms, bms, vs, g, dt, sts, *, reverse, g_off):
    n = g.shape[0]
    mask = _order_mask(n, reverse)
    cum = dot_hi(mask, g)
    tot = jnp.sum(g, axis=0, keepdims=True)
    cum_t, dt_t = cum.T, dt.T
    lane = lax.broadcasted_iota(jnp.int32, (1, LANES), 1)
    sub = lax.broadcasted_iota(jnp.int32, (LANES, 1), 0)
    rep = SSD_HEADS // SSD_GROUPS
    cb = [mm_nt(cms[i], bms[i]) for i in range(SSD_GROUPS)]
    outs, new = [], []
    for h in range(SSD_HEADS):
        pl_, ps_ = (lane == g_off + h).astype(F32), (sub == g_off + h).astype(F32)
        gh = jnp.sum(cum * pl_, axis=1, keepdims=True)
        th = jnp.sum(tot * pl_, axis=1, keepdims=True)
        dt_col = jnp.sum(dt * pl_, axis=1, keepdims=True)
        g_row = jnp.sum(cum_t * ps_, axis=0, keepdims=True)
        dt_row = jnp.sum(dt_t * ps_, axis=0, keepdims=True)
        a = cb[h // rep] * (jnp.exp(jnp.minimum(gh - g_row, 0.0)) * mask * dt_row)
        outs.append(mm_nn(a, vs[h]) + mm_nt(cms[h // rep] * jnp.exp(gh), sts[h]))
        new.append(sts[h] * jnp.exp(th) + mm_tn(vs[h] * (dt_col * jnp.exp(th - gh)), bms[h // rep]))
    return outs, new


def _chunk_const(qs, ks, vs, sts, *, reverse):
    n = qs[0].shape[0]
    r = lax.broadcasted_iota(jnp.int32, (n, n), 0)
    c = lax.broadcasted_iota(jnp.int32, (n, n), 1)
    dist = ((c - r) if reverse else (r - c)).astype(F32)
    row = lax.broadcasted_iota(jnp.int32, (n, 1), 0).astype(F32)
    seen = (n - row) if reverse else (row + 1.0)
    outs, new = [], []
    for h in range(RET_HEADS):
        gm = RET_LOG_GAMMA[h]
        dec = jnp.where(dist >= 0.0, jnp.exp(gm * dist), 0.0)
        a = mm_nt(qs[h], ks[h]) * dec
        outs.append(mm_nn(a, vs[h]) + mm_nt(qs[h] * jnp.exp(gm * seen), sts[h]))
        new.append(sts[h] * math.exp(gm * n) + mm_tn(vs[h], ks[h] * jnp.exp(gm * (n - seen))))
    return outs, new


class ScanCfg:
    def __init__(self, name, fn, heads, dv, st_k, chunk, parts):
        self.name, self.fn, self.heads, self.dv, self.st_k, self.chunk, self.parts = name, fn, heads, dv, st_k, chunk, parts

    def width(self, i):
        return LANES if self.parts[i] is None else self.parts[i][0] * self.parts[i][1]


def _scan_load(cfg, refs):
    vals = []
    for rf, part in zip(refs, cfg.parts):
        if part is None:
            vals.append(rf[...].astype(F32))
        else:
            vals.append([rf[:, i * part[1]:(i + 1) * part[1]].astype(F32) for i in range(part[0])])
    return vals


def _scan_store(cfg, refs, grads):
    for rf, part, g in zip(refs, cfg.parts, grads):
        if part is None:
            rf[...] = g
        else:
            for i in range(part[0]):
                rf[:, i * part[1]:(i + 1) * part[1]] = g[i]


def _chunk_of_step(s, n, nc, reverse):
    if not reverse:
        return s
    return jnp.where(s < nc, nc - 1 - s, n + nc - 1 - s)


def _arr_spec(a, c, pick):
    arr, w, cb = a
    return pl.BlockSpec((c, w), lambda s: (pick(s), cb))


def scan_fwd(cfg, ins, *, t_ctx, reverse, **kw):
    t = ins[0][0].shape[0]
    c, h, ni = cfg.chunk, cfg.heads, len(ins)
    n, nc = t // c, t_ctx // c
    pick = lambda s: _chunk_of_step(s, n, nc, reverse)

    def body(*refs):
        o_ref, st_ref, state = refs[ni:]

        @pl.when(pl.program_id(0) == 0)
        def _():
            state[...] = jnp.zeros_like(state)

        st_ref[...] = state[...]
        outs, new = cfg.fn(*_scan_load(cfg, refs[:ni]), [state[i] for i in range(h)], reverse=reverse, **kw)
        for i in range(h):
            o_ref[:, i * cfg.dv:(i + 1) * cfg.dv] = outs[i]
            state[i] = new[i]

    return pl.pallas_call(
        body, name=f"scan_fwd_{cfg.name}_{'b' if reverse else 'f'}", grid=(n,),
        in_specs=[_arr_spec(a, c, pick) for a in ins],
        out_specs=[pl.BlockSpec((c, h * cfg.dv), lambda s: (pick(s), 0)),
                   pl.BlockSpec((None, h, cfg.dv, cfg.st_k), lambda s: (pick(s), 0, 0, 0))],
        out_shape=[jax.ShapeDtypeStruct((t, h * cfg.dv), F32),
                   jax.ShapeDtypeStruct((n, h, cfg.dv, cfg.st_k), F32)],
        scratch_shapes=[pltpu.VMEM((h, cfg.dv, cfg.st_k), F32)],
        compiler_params=_cparams(("arbitrary",)),
    )(*[a[0] for a in ins])


def scan_bwd(cfg, ins, states, do, *, t_ctx, reverse, **kw):
    t = ins[0][0].shape[0]
    c, h, ni = cfg.chunk, cfg.heads, len(ins)
    n, nc = t // c, t_ctx // c
    pick = lambda s: _chunk_of_step(n - 1 - s, n, nc, reverse)

    def body(*refs):
        st_ref, do_ref = refs[ni:ni + 2]
        grad_refs, dstate = refs[ni + 2:-1], refs[-1]

        @pl.when(pl.program_id(0) == 0)
        def _():
            dstate[...] = jnp.zeros_like(dstate)

        dos = [do_ref[:, i * cfg.dv:(i + 1) * cfg.dv] for i in range(h)]
        _, vjp = jax.vjp(functools.partial(cfg.fn, reverse=reverse, **kw), *_scan_load(cfg, refs[:ni]),
                         [st_ref[i] for i in range(h)])
        grads = vjp((dos, [dstate[i] for i in range(h)]))
        _scan_store(cfg, grad_refs, grads[:-1])
        for i in range(h):
            dstate[i] = grads[-1][i]

    row = lambda w: pl.BlockSpec((c, w), lambda s: (pick(s), 0))
    return pl.pallas_call(
        body, name=f"scan_bwd_{cfg.name}_{'b' if reverse else 'f'}", grid=(n,),
        in_specs=[_arr_spec(a, c, pick) for a in ins]
        + [pl.BlockSpec((None, h, cfg.dv, cfg.st_k), lambda s: (pick(s), 0, 0, 0)), _arr_spec(do, c, pick)],
        out_specs=[row(cfg.width(i)) for i in range(ni)],
        out_shape=[jax.ShapeDtypeStruct((t, cfg.width(i)), F32) for i in range(ni)],
        scratch_shapes=[pltpu.VMEM((h, cfg.dv, cfg.st_k), F32)],
        compiler_params=_cparams(("arbitrary",)),
    )(*[a[0] for a in ins], states, do[0])


def _rb(t_ctx):
    return min(256, t_ctx)


class _Rows:
    def __init__(self, t, t_ctx):
        self.rb = _rb(t_ctx)
        self.nb, self.ncb = t // self.rb, t_ctx // self.rb

    def seg(self, i):
        return jnp.where(i >= self.ncb, 1, 0)

    def rows(self, w, cb=0):
        return pl.BlockSpec((self.rb, w), lambda i: (i, cb))

    def arr(self, a):
        return self.rows(a[1], a[2])

    def const(self, *shape):
        return pl.BlockSpec(shape, lambda i: (0,) * len(shape))

    def per_seg(self, *shape):
        return pl.BlockSpec((None,) + shape, lambda i: (self.seg(i),) + (0,) * len(shape))

    def first(self):
        return pl.program_id(0) == 0

    def seg_first(self):
        i = pl.program_id(0)
        return (i == 0) | (i == self.ncb)


def _acc(ref, val, first):
    @pl.when(first)
    def _():
        ref[...] = val

    @pl.when(jnp.logical_not(first))
    def _():
        ref[...] += val


def _full(a):
    return (a, a.shape[1], 0)


def _norm_mod(x, nw, sh, sc):
    y = x * lax.rsqrt(jnp.mean(x * x, axis=-1, keepdims=True) + RMS_EPS)
    return (y * nw) * (1.0 + sc) + sh


def _gated_norm(z, gt, pw):
    y = z * lax.rsqrt(jnp.mean(z * z, axis=-1, keepdims=True) + RMS_EPS)
    return gt * (y * pw)


def _swiglu(g, u):
    return _silu(g) * u


def nm_matmul(x, nw, mod, w, *, sh, sc, t_ctx, out_dtype, name):
    t, n = x.shape[0], w.shape[1]
    r = _Rows(t, t_ctx)

    def body(x_ref, nw_ref, mod_ref, w_ref, o_ref, h_ref):
        h = _norm_mod(x_ref[...], nw_ref[...], mod_ref[sh:sh + 1, :], mod_ref[sc:sc + 1, :]).astype(BF16)
        h_ref[...] = h
        for j in range(n // TN_CHUNK):
            sl = slice(j * TN_CHUNK, (j + 1) * TN_CHUNK)
            o_ref[:, sl] = _dot(h, w_ref[:, sl], 1, 0).astype(out_dtype)

    return pl.pallas_call(
        body, name=name, grid=(r.nb,),
        in_specs=[r.rows(D), r.const(1, D), r.per_seg(6, D), r.const(D, n)],
        out_specs=[r.rows(n), r.rows(D)],
        out_shape=[jax.ShapeDtypeStruct((t, n), out_dtype), jax.ShapeDtypeStruct((t, D), BF16)],
        compiler_params=_cparams(("arbitrary",)),
    )(x, nw, mod, w)


def nm_bwd(dout, w, x, nw, mod, dres, *, sh, sc, t_ctx, name):
    t, n = x.shape[0], w.shape[1]
    r = _Rows(t, t_ctx)

    def body(do_ref, w_ref, x_ref, nw_ref, mod_ref, dres_ref, dx_ref, dnw_ref, dsh_ref, dsc_ref):
        dh = jnp.zeros((r.rb, D), F32)
        for j in range(n // TN_CHUNK):
            sl = slice(j * TN_CHUNK, (j + 1) * TN_CHUNK)
            dh = dh + _dot(do_ref[:, sl], w_ref[:, sl], 1, 1)
        _, vjp = jax.vjp(_norm_mod, x_ref[...], nw_ref[...], mod_ref[sh:sh + 1, :], mod_ref[sc:sc + 1, :])
        dx, dnw, dsh, dsc = vjp(dh)
        dx_ref[...] = dx + dres_ref[...]
        _acc(dnw_ref, dnw, r.first())
        _acc(dsh_ref, dsh, r.seg_first())
        _acc(dsc_ref, dsc, r.seg_first())

    return pl.pallas_call(
        body, name=name, grid=(r.nb,),
        in_specs=[r.rows(n), r.const(D, n), r.rows(D), r.const(1, D), r.per_seg(6, D), r.rows(D)],
        out_specs=[r.rows(D), r.const(1, D), r.per_seg(1, D), r.per_seg(1, D)],
        out_shape=[jax.ShapeDtypeStruct((t, D), F32), jax.ShapeDtypeStruct((1, D), F32),
                   jax.ShapeDtypeStruct((2, 1, D), F32), jax.ShapeDtypeStruct((2, 1, D), F32)],
        compiler_params=_cparams(("arbitrary",)),
    )(dout, w, x, nw, mod, dres)


def mm_postnorm(a, w, xres, mod, pw, *, gt, swiglu, t_ctx, name):
    t, k = xres.shape[0], w.shape[0]
    r = _Rows(t, t_ctx)

    def body(a_ref, w_ref, x_ref, mod_ref, pw_ref, xn_ref, z_ref, *act_ref):
        if swiglu:
            act = _swiglu(a_ref[:, :k].astype(F32), a_ref[:, k:].astype(F32)).astype(BF16)
            act_ref[0][...] = act
        else:
            act = a_ref[...]
        z = _dot(act, w_ref[...], 1, 0)
        z_ref[...] = z
        xn_ref[...] = x_ref[...] + _gated_norm(z, mod_ref[gt:gt + 1, :], pw_ref[...])

    outs = [jax.ShapeDtypeStruct((t, D), F32), jax.ShapeDtypeStruct((t, D), F32)]
    ospecs = [r.rows(D), r.rows(D)]
    if swiglu:
        outs.append(jax.ShapeDtypeStruct((t, k), BF16))
        ospecs.append(r.rows(k))
    return pl.pallas_call(
        body, name=name, grid=(r.nb,),
        in_specs=[r.rows(a.shape[1]), r.const(k, D), r.rows(D), r.per_seg(6, D), r.const(1, D)],
        out_specs=ospecs, out_shape=outs,
        compiler_params=_cparams(("arbitrary",)),
    )(a, w, xres, mod, pw)


def postnorm_bwd(dxn, z, mod, pw, w, u13, *, gt, t_ctx, name):
    t, k = z.shape[0], w.shape[0]
    r = _Rows(t, t_ctx)
    swiglu = u13 is not None

    def body(dxn_ref, z_ref, mod_ref, pw_ref, w_ref, *rest):
        if swiglu:
            u_ref, da_ref, dz_ref, dgt_ref, dpw_ref = rest
        else:
            da_ref, dz_ref, dgt_ref, dpw_ref = rest
        _, vjp = jax.vjp(_gated_norm, z_ref[...], mod_ref[gt:gt + 1, :], pw_ref[...])
        dz, dgt, dpw = vjp(dxn_ref[...])
        dzb = dz.astype(BF16)
        dz_ref[...] = dzb
        da = _dot(dzb, w_ref[...], 1, 1)
        if swiglu:
            _, vjp2 = jax.vjp(_swiglu, u_ref[:, :k].astype(F32), u_ref[:, k:].astype(F32))
            dg, du = vjp2(da)
            da_ref[:, :k] = dg.astype(BF16)
            da_ref[:, k:] = du.astype(BF16)
        else:
            da_ref[...] = da
        _acc(dgt_ref, dgt, r.seg_first())
        _acc(dpw_ref, dpw, r.first())

    ins = [dxn, z, mod, pw, w] + ([u13] if swiglu else [])
    in_specs = [r.rows(D), r.rows(D), r.per_seg(6, D), r.const(1, D), r.const(k, D)] + ([r.rows(2 * k)] if swiglu else [])
    da_shape = jax.ShapeDtypeStruct((t, 2 * k), BF16) if swiglu else jax.ShapeDtypeStruct((t, k), F32)
    return pl.pallas_call(
        body, name=name, grid=(r.nb,), in_specs=in_specs,
        out_specs=[r.rows(2 * k if swiglu else k), r.rows(D), r.per_seg(1, D), r.const(1, D)],
        out_shape=[da_shape, jax.ShapeDtypeStruct((t, D), BF16), jax.ShapeDtypeStruct((2, 1, D), F32),
                   jax.ShapeDtypeStruct((1, D), F32)],
        compiler_params=_cparams(("arbitrary",)),
    )(*ins)


def grad_matmul(a, b, *, by_shard, name):
    t, k = a.shape
    n = b.shape[1]
    tt = t // 4
    tn = n // 4 if by_shard else (TN_CHUNK if n % TN_CHUNK == 0 else n)
    tk = k if k <= 1024 else k // 2

    def body(a_ref, b_ref, o_ref, acc):
        s = pl.program_id(2)

        @pl.when(s == 0)
        def _():
            acc[...] = jnp.zeros_like(acc)

        acc[...] += _dot(a_ref[...], b_ref[...], 0, 0)

        @pl.when(s == pl.num_programs(2) - 1)
        def _():
            o_ref[...] = acc[...].astype(BF16)

    if by_shard:
        out_spec = pl.BlockSpec((None, tk, tn), lambda i, j, s: (j, i, 0))
        out_shape = jax.ShapeDtypeStruct((4, k, tn), BF16)
    else:
        out_spec = pl.BlockSpec((tk, tn), lambda i, j, s: (i, j))
        out_shape = jax.ShapeDtypeStruct((k, n), BF16)
    return pl.pallas_call(
        body, name=name, grid=(k // tk, n // tn, t // tt),
        in_specs=[pl.BlockSpec((tt, tk), lambda i, j, s: (s, i)), pl.BlockSpec((tt, tn), lambda i, j, s: (s, j))],
        out_specs=out_spec, out_shape=out_shape, scratch_shapes=[pltpu.VMEM((tk, tn), F32)],
        compiler_params=_cparams(("parallel", "parallel", "arbitrary")),
    )(a, b)


def _shifted(cur, prev, nxt, d, has_prev, has_next):
    n = cur.shape[0]
    row = lax.broadcasted_iota(jnp.int32, cur.shape, 0)
    if d == 0:
        return cur
    if d < 0:
        return jnp.where(row < -d, pltpu.roll(prev, -d, 0) * has_prev, pltpu.roll(cur, -d, 0))
    return jnp.where(row >= n - d, pltpu.roll(nxt, n - d, 0) * has_next, pltpu.roll(cur, n - d, 0))


def _halo_specs(r, w, cb):
    return [pl.BlockSpec((r.rb, w), lambda i: (jnp.maximum(i - 1, 0), cb)),
            pl.BlockSpec((r.rb, w), lambda i: (i, cb)),
            pl.BlockSpec((r.rb, w), lambda i: (jnp.minimum(i + 1, r.nb - 1), cb))]


def _halo_flags(r):
    i = pl.program_id(0)
    has_prev = ((i != 0) & (i != r.ncb)).astype(F32)
    has_next = ((i != r.ncb - 1) & (i != r.nb - 1)).astype(F32)
    return has_prev, has_next


def conv_fwd(p, w8, b, *, t_ctx, name):
    t = p.shape[0]
    r = _Rows(t, t_ctx)
    cw = 1024

    def body(pp_ref, pc_ref, pn_ref, w_ref, b_ref, u_ref):
        hp, hn = _halo_flags(r)
        prev, cur, nxt = pp_ref[...], pc_ref[...], pn_ref[...]
        acc = jnp.broadcast_to(b_ref[...], cur.shape)
        for kk in range(5):
            acc = acc + w_ref[kk:kk + 1, :] * _shifted(cur, prev, nxt, kk - 2, hp, hn)
        u_ref[...] = acc

    return pl.pallas_call(
        body, name=name, grid=(r.nb,),
        in_specs=_halo_specs(r, cw, P_XBC // cw) + [r.const(8, cw), r.const(1, cw)],
        out_specs=r.rows(cw), out_shape=jax.ShapeDtypeStruct((t, cw), F32),
        compiler_params=_cparams(("arbitrary",)),
    )(p, p, p, w8, b)


def conv_bwd(p, du, w8, *, t_ctx, name):
    t = p.shape[0]
    r = _Rows(t, t_ctx)
    cw = 1024

    def body(pp_ref, pc_ref, pn_ref, dp_ref, dc_ref, dn_ref, w_ref, dx_ref, dw_ref, db_ref):
        hp, hn = _halo_flags(r)
        prev, cur, nxt = pp_ref[...], pc_ref[...], pn_ref[...]
        dprev, dcur, dnxt = dp_ref[...], dc_ref[...], dn_ref[...]
        acc = jnp.zeros_like(dcur)
        for kk in range(5):
            acc = acc + w_ref[kk:kk + 1, :] * _shifted(dcur, dprev, dnxt, 2 - kk, hp, hn)
        dx_ref[...] = acc
        first = r.first()
        for kk in range(5):
            _acc(dw_ref.at[kk:kk + 1, :], jnp.sum(_shifted(cur, prev, nxt, kk - 2, hp, hn) * dcur, axis=0, keepdims=True),
                 first)
        _acc(dw_ref.at[5:8, :], jnp.zeros((3, cw), F32), first)
        _acc(db_ref, jnp.sum(dcur, axis=0, keepdims=True), first)

    return pl.pallas_call(
        body, name=name, grid=(r.nb,),
        in_specs=_halo_specs(r, cw, P_XBC // cw) + _halo_specs(r, cw, 0) + [r.const(8, cw)],
        out_specs=[r.rows(cw), r.const(8, cw), r.const(1, cw)],
        out_shape=[jax.ShapeDtypeStruct((t, cw), F32), jax.ShapeDtypeStruct((8, cw), F32),
                   jax.ShapeDtypeStruct((1, cw), F32)],
        compiler_params=_cparams(("arbitrary",)),
    )(p, p, p, du, du, du, w8)


def _ssd_act(u_x, u_b0, u_b1, u_c0, u_c1, dtraw, dtb, alog):
    dt = _softplus(dtraw + dtb)
    return _silu(u_x), _silu(u_b0), _silu(u_b1), _silu(u_c0), _silu(u_c1), dt, dt * (-jnp.exp(alog))


def _ssd_act_inputs(u_ref, dt_ref, dtb_ref, alog_ref):
    return (u_ref[:, 0:512], u_ref[:, 512:640], u_ref[:, 640:768], u_ref[:, 768:896], u_ref[:, 896:1024],
            dt_ref[...], dtb_ref[...], alog_ref[...])


_SSD_ACT_OFFSETS = ((0, 512), (512, 128), (640, 128), (768, 128), (896, 128))


def ssd_act_fwd(u, p, dtb, alog, *, t_ctx, name):
    t = u.shape[0]
    r = _Rows(t, t_ctx)

    def body(u_ref, dt_ref, dtb_ref, alog_ref, act_ref, dto_ref, g_ref):
        *pieces, dt, g = _ssd_act(*_ssd_act_inputs(u_ref, dt_ref, dtb_ref, alog_ref))
        for (off, w), val in zip(_SSD_ACT_OFFSETS, pieces):
            act_ref[:, off:off + w] = val
        dto_ref[...] = dt
        g_ref[...] = g

    return pl.pallas_call(
        body, name=name, grid=(r.nb,),
        in_specs=[r.rows(1024), r.rows(LANES, P_DT // LANES), r.const(1, LANES), r.const(1, LANES)],
        out_specs=[r.rows(1024), r.rows(LANES), r.rows(LANES)],
        out_shape=[jax.ShapeDtypeStruct((t, 1024), F32), jax.ShapeDtypeStruct((t, LANES), F32),
                   jax.ShapeDtypeStruct((t, LANES), F32)],
        compiler_params=_cparams(("arbitrary",)),
    )(u, p, dtb, alog)


def ssd_act_bwd(u, p, dtb, alog, dxs_list, db_list, dc_list, ddt_list, dg_list, *, t_ctx, name):
    t = u.shape[0]
    r = _Rows(t, t_ctx)
    lists = (dxs_list, db_list, dc_list, ddt_list, dg_list)
    widths = (512, 256, 256, LANES, LANES)

    def body(*refs):
        u_ref, dt_ref, dtb_ref, alog_ref = refs[:4]
        i, sums = 4, []
        for lst in lists:
            sums.append(sum(rf[...] for rf in refs[i:i + len(lst)]))
            i += len(lst)
        du_ref, ddt_ref, ddtb_ref, dalog_ref = refs[i:]
        dxs, dbm, dcm, ddt_ct, dg_ct = sums
        _, vjp = jax.vjp(_ssd_act, *_ssd_act_inputs(u_ref, dt_ref, dtb_ref, alog_ref))
        *dpieces, ddt, ddtb, dalog = vjp((dxs, dbm[:, 0:128], dbm[:, 128:256], dcm[:, 0:128], dcm[:, 128:256],
                                          ddt_ct, dg_ct))
        for (off, w), val in zip(_SSD_ACT_OFFSETS, dpieces):
            du_ref[:, off:off + w] = val
        ddt_ref[...] = ddt
        _acc(ddtb_ref, ddtb, r.first())
        _acc(dalog_ref, dalog, r.first())

    ins = [u, p, dtb, alog] + [a for lst in lists for a in lst]
    in_specs = ([r.rows(1024), r.rows(LANES, P_DT // LANES), r.const(1, LANES), r.const(1, LANES)]
                + [r.rows(w) for lst, w in zip(lists, widths) for _ in lst])
    return pl.pallas_call(
        body, name=name, grid=(r.nb,), in_specs=in_specs,
        out_specs=[r.rows(1024), r.rows(LANES), r.const(1, LANES), r.const(1, LANES)],
        out_shape=[jax.ShapeDtypeStruct((t, 1024), F32), jax.ShapeDtypeStruct((t, LANES), F32),
                   jax.ShapeDtypeStruct((1, LANES), F32), jax.ShapeDtypeStruct((1, LANES), F32)],
        compiler_params=_cparams(("arbitrary",)),
    )(*ins)


GLA_RANK = 16


def _gla_pre(q, lr, ups_f, ups_b, bf, bb):
    lane = lax.broadcasted_iota(jnp.int32, (1, LANES), 1)
    zf, zb = bf, bb
    for i in range(GLA_RANK):
        zf = zf + jnp.sum(lr * (lane == i).astype(F32), axis=1, keepdims=True) * ups_f[i]
        zb = zb + jnp.sum(lr * (lane == GLA_RANK + i).astype(F32), axis=1, keepdims=True) * ups_b[i]
    return q * (GLA_DK ** -0.5), _log_sigmoid(zf) * (1.0 / 16.0), _log_sigmoid(zb) * (1.0 / 16.0)


def _gla_pre_inputs(q_ref, lr_ref, up_ref, b_ref):
    rows = lambda d: [up_ref[d, i:i + 1, :] for i in range(GLA_RANK)]
    return q_ref[...], lr_ref[...], rows(0), rows(1), b_ref[0], b_ref[1]


def _gla_pre_specs(r):
    return [r.rows(LANES, P_GQ // LANES), r.rows(LANES, P_LR // LANES), r.const(2, GLA_RANK, LANES),
            r.const(2, 1, LANES)]


def gla_pre_fwd(p, up, gbias, *, t_ctx, name):
    t = p.shape[0]
    r = _Rows(t, t_ctx)

    def body(q_ref, lr_ref, up_ref, b_ref, qs_ref, gf_ref, gb_ref):
        qs, gf, gb = _gla_pre(*_gla_pre_inputs(q_ref, lr_ref, up_ref, b_ref))
        qs_ref[...] = qs
        gf_ref[...] = gf
        gb_ref[...] = gb

    sd = jax.ShapeDtypeStruct((t, LANES), F32)
    return pl.pallas_call(
        body, name=name, grid=(r.nb,), in_specs=_gla_pre_specs(r),
        out_specs=[r.rows(LANES)] * 3, out_shape=[sd, sd, sd],
        compiler_params=_cparams(("arbitrary",)),
    )(p, p, up, gbias)


def gla_pre_bwd(p, up, gbias, dq_list, dgf, dgb, *, t_ctx, name):
    t = p.shape[0]
    r = _Rows(t, t_ctx)
    nq = len(dq_list)

    def body(*refs):
        q_ref, lr_ref, up_ref, b_ref = refs[:4]
        dq = sum(rf[...] for rf in refs[4:4 + nq])
        dgf_ref, dgb_ref, dqo_ref, dlr_ref, dup_ref, db_ref = refs[4 + nq:]
        _, vjp = jax.vjp(_gla_pre, *_gla_pre_inputs(q_ref, lr_ref, up_ref, b_ref))
        dqo, dlr, dups_f, dups_b, dbf, dbb = vjp((dq, dgf_ref[...], dgb_ref[...]))
        dqo_ref[...] = dqo
        dlr_ref[...] = dlr

        def write(add):
            for d, (dups, dbias) in enumerate(((dups_f, dbf), (dups_b, dbb))):
                for i in range(GLA_RANK):
                    dup_ref[d, i:i + 1, :] = dups[i] + (dup_ref[d, i:i + 1, :] if add else 0.0)
                db_ref[d] = dbias + (db_ref[d] if add else 0.0)

        first = r.first()
        pl.when(first)(lambda: write(False))
        pl.when(jnp.logical_not(first))(lambda: write(True))

    sd = jax.ShapeDtypeStruct((t, LANES), F32)
    return pl.pallas_call(
        body, name=name, grid=(r.nb,), in_specs=_gla_pre_specs(r) + [r.rows(LANES)] * (nq + 2),
        out_specs=[r.rows(LANES), r.rows(LANES), r.const(2, GLA_RANK, LANES), r.const(2, 1, LANES)],
        out_shape=[sd, sd, jax.ShapeDtypeStruct((2, GLA_RANK, LANES), F32), jax.ShapeDtypeStruct((2, 1, LANES), F32)],
        compiler_params=_cparams(("arbitrary",)),
    )(p, p, up, gbias, *dq_list, dgf, dgb)


def _swap_halves(v):
    lane = lax.broadcasted_iota(jnp.int32, v.shape, 1)
    w = v.shape[1]
    return jnp.where((lane & 63) >= 32, pltpu.roll(v, 32, 1), pltpu.roll(v, w - 32, 1))


def rope_pair(q_list, k_list, cos, sin, *, transpose, t_ctx, name):
    t = cos.shape[0]
    r = _Rows(t, t_ctx)
    nq, nk = len(q_list), len(k_list)
    w = RET_HEADS * RET_DH

    def body(*refs):
        q = sum(rf[...] for rf in refs[:nq])
        k = sum(rf[...] for rf in refs[nq:nq + nk])
        cos_ref, sin_ref, qo_ref, ko_ref = refs[nq + nk:]
        cs, sn = cos_ref[...], sin_ref[...]
        if transpose:
            rot = lambda v: v * cs + _swap_halves(v * sn)
        else:
            rot = lambda v: v * cs + _swap_halves(v) * sn
        qo_ref[...] = rot(q) * (RET_DH ** -0.5)
        ko_ref[...] = rot(k)

    sd = jax.ShapeDtypeStruct((t, w), F32)
    return pl.pallas_call(
        body, name=name, grid=(r.nb,),
        in_specs=[r.arr(a) for a in list(q_list) + list(k_list)] + [r.rows(w), r.rows(w)],
        out_specs=[r.rows(w), r.rows(w)], out_shape=[sd, sd],
        compiler_params=_cparams(("arbitrary",)),
    )(*[a[0] for a in list(q_list) + list(k_list)], cos, sin)


def _gla_post(o_f, o_b, r_gate, nw, pm):
    o = o_f + o_b
    ms = dot_hi(o * o, pm)
    return o * lax.rsqrt(ms + RMS_EPS) * nw * _silu(r_gate)


def _ssd_post(o_f, o_b, xs, z, dsk, nw):
    y = (o_f + o_b + dsk * xs) * _silu(z)
    return y * lax.rsqrt(jnp.mean(y * y, axis=-1, keepdims=True) + RMS_EPS) * nw


def _ret_post(o_f, o_b, gate, nw, pm):
    o = o_f + o_b
    xc = o - dot_hi(o, pm)
    var = dot_hi(xc * xc, pm)
    return xc * lax.rsqrt(var + RMS_EPS) * nw * _silu(gate)


def _mix_post_specs(r):
    return [r.rows(256), r.rows(256), r.rows(512), r.rows(512), r.rows(256), r.rows(256),
            r.rows(256, P_GR // 256), r.rows(512, P_Z // 512), r.rows(256, P_RG // 256),
            r.rows(512, 0),
            r.const(1, 256), r.const(1, 512), r.const(1, 512), r.const(1, 256), r.const(256, 256)]


def mix_post_fwd(go_f, go_b, so_f, so_b, ro_f, ro_b, p, act, gla_nw, dsk, ssd_nw, ret_nw, pm, *, t_ctx, name):
    t = p.shape[0]
    r = _Rows(t, t_ctx)

    def body(gf, gb, sf, sb, rf, rbk, rg, z, gg, xs, gnw, dk, snw, rnw, pmr, y_ref):
        y_ref[:, 0:256] = _gla_post(gf[...], gb[...], rg[...], gnw[...], pmr[...]).astype(BF16)
        y_ref[:, 256:768] = _ssd_post(sf[...], sb[...], xs[...], z[...], dk[...], snw[...]).astype(BF16)
        y_ref[:, 768:1024] = _ret_post(rf[...], rbk[...], gg[...], rnw[...], pmr[...]).astype(BF16)

    return pl.pallas_call(
        body, name=name, grid=(r.nb,), in_specs=_mix_post_specs(r),
        out_specs=r.rows(D), out_shape=jax.ShapeDtypeStruct((t, D), BF16),
        compiler_params=_cparams(("arbitrary",)),
    )(go_f, go_b, so_f, so_b, ro_f, ro_b, p, p, p, act, gla_nw, dsk, ssd_nw, ret_nw, pm)


def mix_post_bwd(go_f, go_b, so_f, so_b, ro_f, ro_b, p, act, gla_nw, dsk, ssd_nw, ret_nw, pm, dy, *, t_ctx, name):
    t = p.shape[0]
    r = _Rows(t, t_ctx)

    def body(gf, gb, sf, sb, rf, rbk, rg, z, gg, xs, gnw, dk, snw, rnw, pmr, dy_ref,
             dgo, dso, dro, drg, dz, dgg, dxs, dgnw, ddk, dsnw, drnw):
        first = r.first()
        pmv = pmr[...]
        _, vjp = jax.vjp(lambda a, b, c, d: _gla_post(a, b, c, d, pmv), gf[...], gb[...], rg[...], gnw[...])
        a, _, c, d = vjp(dy_ref[:, 0:256])
        dgo[...] = a
        drg[...] = c
        _acc(dgnw, d, first)
        _, vjp = jax.vjp(_ssd_post, sf[...], sb[...], xs[...], z[...], dk[...], snw[...])
        a, _, c, d, e, f = vjp(dy_ref[:, 256:768])
        dso[...] = a
        dxs[...] = c
        dz[...] = d
        _acc(ddk, e, first)
        _acc(dsnw, f, first)
        _, vjp = jax.vjp(lambda a, b, c, d: _ret_post(a, b, c, d, pmv), rf[...], rbk[...], gg[...], rnw[...])
        a, _, c, d = vjp(dy_ref[:, 768:1024])
        dro[...] = a
        dgg[...] = c
        _acc(drnw, d, first)

    sd = lambda w: jax.ShapeDtypeStruct((t, w), F32)
    sp = lambda w: jax.ShapeDtypeStruct((1, w), F32)
    return pl.pallas_call(
        body, name=name, grid=(r.nb,), in_specs=_mix_post_specs(r) + [r.rows(D)],
        out_specs=[r.rows(256), r.rows(512), r.rows(256), r.rows(256), r.rows(512), r.rows(256), r.rows(512),
                   r.const(1, 256), r.const(1, 512), r.const(1, 512), r.const(1, 256)],
        out_shape=[sd(256), sd(512), sd(256), sd(256), sd(512), sd(256), sd(512), sp(256), sp(512), sp(512), sp(256)],
        compiler_params=_cparams(("arbitrary",)),
    )(go_f, go_b, so_f, so_b, ro_f, ro_b, p, p, p, act, gla_nw, dsk, ssd_nw, ret_nw, pm, dy)


def dp_assemble(pieces, *, t_ctx, name):
    t = pieces[0][2][0].shape[0]
    r = _Rows(t, t_ctx)
    flat = [a for _, _, arrs in pieces for a in arrs]

    def body(*refs):
        o_ref = refs[-1]
        i = 0
        for start, w, arrs in pieces:
            tot = sum(refs[i + j][...] for j in range(len(arrs)))
            i += len(arrs)
            o_ref[:, start:start + w] = tot.astype(BF16)

    return pl.pallas_call(
        body, name=name, grid=(r.nb,),
        in_specs=[r.rows(w) for _, w, arrs in pieces for _ in arrs],
        out_specs=r.rows(NP), out_shape=jax.ShapeDtypeStruct((t, NP), BF16),
        compiler_params=_cparams(("arbitrary",)),
    )(*flat)


def loss_head(xs, target, *, t_ctx, name):
    t = xs.shape[0]
    r = _Rows(t, t_ctx)

    def body(x_ref, t_ref, l_ref, dx_ref):
        i = pl.program_id(0)
        lat = (i >= r.ncb).astype(F32)
        diff = (x_ref[...] - t_ref[...]) * lat
        dx_ref[...] = diff * (1.0 / D)
        part = jnp.sum(jnp.sum(diff * diff, axis=1, keepdims=True), axis=0, keepdims=True) * (0.5 / D)
        _acc(l_ref, jnp.broadcast_to(part, (1, LANES)), r.first())

    return pl.pallas_call(
        body, name=name, grid=(r.nb,),
        in_specs=[r.rows(D), pl.BlockSpec((r.rb, D), lambda i: (jnp.maximum(i - r.ncb, 0), 0))],
        out_specs=[r.const(1, LANES), r.rows(D)],
        out_shape=[jax.ShapeDtypeStruct((1, LANES), F32), jax.ShapeDtypeStruct((t, D), F32)],
        compiler_params=_cparams(("arbitrary",)),
    )(xs, target)


def _ada(cg, w):
    return dot_hi_plain(_silu(cg), w)


def ada_fwd(cg, ada_w, *, name):
    nl, _, n = ada_w.shape

    def body(c_ref, w_ref, o_ref):
        o_ref[...] = _ada(c_ref[...], w_ref[...])

    return pl.pallas_call(
        body, name=name, grid=(nl,),
        in_specs=[pl.BlockSpec((16, D), lambda l: (0, 0)), pl.BlockSpec((None, D, n), lambda l: (l, 0, 0))],
        out_specs=pl.BlockSpec((None, 16, n), lambda l: (l, 0, 0)),
        out_shape=jax.ShapeDtypeStruct((nl, 16, n), F32),
        compiler_params=_cparams(("arbitrary",)),
    )(cg, ada_w)


def ada_bwd(cg, ada_w, dmod, *, name):
    nl, _, n = ada_w.shape

    def body(c_ref, w_ref, g_ref, dw_ref, dc_ref):
        _, vjp = jax.vjp(_ada, c_ref[...], w_ref[...])
        dc, dw = vjp(g_ref[...])
        dw_ref[...] = dw
        _acc(dc_ref, dc, pl.program_id(0) == 0)

    return pl.pallas_call(
        body, name=name, grid=(nl,),
        in_specs=[pl.BlockSpec((16, D), lambda l: (0, 0)), pl.BlockSpec((None, D, n), lambda l: (l, 0, 0)),
                  pl.BlockSpec((None, 16, n), lambda l: (l, 0, 0))],
        out_specs=[pl.BlockSpec((None, D, n), lambda l: (l, 0, 0)), pl.BlockSpec((16, D), lambda l: (0, 0))],
        out_shape=[jax.ShapeDtypeStruct((nl, D, n), F32), jax.ShapeDtypeStruct((16, D), F32)],
        compiler_params=_cparams(("arbitrary",)),
    )(cg, ada_w, dmod)


def _row_block(rows):
    for br in range(512, 7, -8):
        if rows % br == 0:
            return br
    return rows


def adamw(w, g, m, v, *, name):
    rows, cols = w.shape
    br = _row_block(rows)

    def body(w_ref, g_ref, m_ref, v_ref, d_ref, nm_ref, nv_ref):
        gg = g_ref[...]
        nm = ADAM_B1 * m_ref[...] + (1.0 - ADAM_B1) * gg
        nv = ADAM_B2 * v_ref[...] + (1.0 - ADAM_B2) * (gg * gg)
        m_hat = nm / (1.0 - ADAM_B1 ** ADAM_STEP)
        v_hat = nv / (1.0 - ADAM_B2 ** ADAM_STEP)
        d_ref[...] = -ADAM_LR * (m_hat / (jnp.sqrt(v_hat) + ADAM_EPS) + ADAM_WD * w_ref[...])
        nm_ref[...] = nm
        nv_ref[...] = nv

    spec = pl.BlockSpec((br, cols), lambda i: (i, 0))
    sd = jax.ShapeDtypeStruct((rows, cols), F32)
    return pl.pallas_call(
        body, name=name, grid=(rows // br,), in_specs=[spec] * 4, out_specs=[spec] * 3, out_shape=[sd] * 3,
        compiler_params=_cparams(("parallel",)),
    )(w, g, m, v)


def sum_leading(a, out_dtype, *, name):
    n, rows, cols = a.shape
    br = _row_block(rows)

    def body(a_ref, o_ref):
        acc = a_ref[0].astype(F32)
        for i in range(1, n):
            acc = acc + a_ref[i].astype(F32)
        o_ref[...] = acc.astype(out_dtype)

    return pl.pallas_call(
        body, name=name, grid=(rows // br,),
        in_specs=[pl.BlockSpec((n, br, cols), lambda i: (0, i, 0))],
        out_specs=pl.BlockSpec((br, cols), lambda i: (i, 0)),
        out_shape=jax.ShapeDtypeStruct((rows, cols), out_dtype),
        compiler_params=_cparams(("parallel",)),
    )(a)


def _position():
    return lax.axis_index("x"), lax.axis_index("y"), lax.axis_index("c")


def _other_chips(x, y):
    return [(1 - x, y), (x, 1 - y), (1 - x, 1 - y)]


def all_gather8(blk, *, name):
    m_per, n = blk.shape

    def body(x_ref, out_ref, send_sems, recv_sems, local_sem):
        x, y, c = _position()
        me, sibling = (x, y, c), (x, y, 1 - c)
        chips = _other_chips(x, y)

        def rows(px, py, pc):
            return out_ref.at[pl.ds((4 * px + 2 * py + pc) * m_per, m_per), :]

        def copy(k, block, to, src=None):
            return pltpu.make_async_remote_copy(
                src_ref=rows(*block) if src is None else src, dst_ref=rows(*block),
                send_sem=send_sems.at[k], recv_sem=recv_sems.at[k], device_id=to, device_id_type=MESH)

        mine = pltpu.make_async_copy(x_ref, rows(*me), local_sem)
        mine.start()
        first = [copy(0, me, sibling, src=x_ref)]
        first += [copy(1 + j, me, (*chip, c), src=x_ref) for j, chip in enumerate(chips)]
        for cp in first:
            cp.start()
        passed = [copy(4 + j, (*chip, c), sibling) for j, chip in enumerate(chips)]
        for j, chip in enumerate(chips):
            copy(1 + j, (*chip, c), me).wait_recv()
            passed[j].start()
        copy(0, sibling, me).wait_recv()
        for j, chip in enumerate(chips):
            copy(4 + j, (*chip, 1 - c), me).wait_recv()
        for cp in first + passed:
            cp.wait_send()
        mine.wait()

    return pl.pallas_call(
        body, name=name,
        out_shape=jax.ShapeDtypeStruct((8 * m_per, n), blk.dtype),
        in_specs=[pl.BlockSpec(memory_space=pltpu.VMEM)],
        out_specs=pl.BlockSpec(memory_space=pltpu.VMEM),
        scratch_shapes=[pltpu.SemaphoreType.DMA((7,)), pltpu.SemaphoreType.DMA((7,)), pltpu.SemaphoreType.DMA],
        compiler_params=pltpu.CompilerParams(vmem_limit_bytes=VMEM_LIMIT),
    )(blk)


_ANY = pl.BlockSpec(memory_space=pl.ANY)


def pair_exchange(arrs, *, name):
    n = len(arrs)

    def body(*refs):
        ins, outs, send_sems, recv_sems = refs[:n], refs[n:2 * n], refs[2 * n], refs[2 * n + 1]
        x, y, c = _position()
        cps = [pltpu.make_async_remote_copy(src_ref=ins[k], dst_ref=outs[k], send_sem=send_sems.at[k],
                                            recv_sem=recv_sems.at[k], device_id=(x, y, 1 - c), device_id_type=MESH)
               for k in range(n)]
        for cp in cps:
            cp.start()
        for cp in cps:
            cp.wait()

    return pl.pallas_call(
        body, name=name, out_shape=[jax.ShapeDtypeStruct(a.shape, a.dtype) for a in arrs],
        in_specs=[_ANY] * n, out_specs=[_ANY] * n,
        scratch_shapes=[pltpu.SemaphoreType.DMA((n,)), pltpu.SemaphoreType.DMA((n,))],
    )(*arrs)


def chip_exchange(arrs, *, gather, name):
    n = len(arrs)

    def body(*refs):
        ins, outs = refs[:n], refs[n:2 * n]
        send_sems, recv_sems, local_sems = refs[2 * n:]
        x, y, c = _position()
        me = 2 * x + y
        chips = _other_chips(x, y)
        local = [pltpu.make_async_copy(ins[k] if gather else ins[k].at[me], outs[k].at[me], local_sems.at[k])
                 for k in range(n)]
        for cp in local:
            cp.start()
        cps = []
        for k in range(n):
            for j, (px, py) in enumerate(chips):
                peer = 2 * px + py
                cps.append(pltpu.make_async_remote_copy(
                    src_ref=ins[k] if gather else ins[k].at[peer], dst_ref=outs[k].at[me],
                    send_sem=send_sems.at[3 * k + j], recv_sem=recv_sems.at[3 * k + j],
                    device_id=(px, py, c), device_id_type=MESH))
        for cp in cps:
            cp.start()
        for k in range(n):
            for j, (px, py) in enumerate(chips):
                peer = 2 * px + py
                pltpu.make_async_remote_copy(
                    src_ref=ins[k] if gather else ins[k].at[peer], dst_ref=outs[k].at[peer],
                    send_sem=send_sems.at[3 * k + j], recv_sem=recv_sems.at[3 * k + j],
                    device_id=(px, py, c), device_id_type=MESH).wait_recv()
        for cp in cps:
            cp.wait_send()
        for cp in local:
            cp.wait()

    out_shape = [jax.ShapeDtypeStruct(((4,) + a.shape) if gather else a.shape, a.dtype) for a in arrs]
    return pl.pallas_call(
        body, name=name, out_shape=out_shape, in_specs=[_ANY] * n, out_specs=[_ANY] * n,
        scratch_shapes=[pltpu.SemaphoreType.DMA((3 * n,)), pltpu.SemaphoreType.DMA((3 * n,)),
                        pltpu.SemaphoreType.DMA((n,))],
    )(*arrs)


_HBM = pl.BlockSpec(memory_space=pltpu.HBM)
_SEM = pl.BlockSpec(memory_space=pltpu.SEMAPHORE)
_DATAFLOW = pltpu.SideEffectType.DATAFLOW_SIDE_EFFECTING


def _half_rows(ref_rows, c):
    return pl.ds(c * (ref_rows // 2), ref_rows // 2)


def _gather_copies(shards, zones, send_sems, recv_sems, as_receiver):
    x, y, c = _position()
    me = 2 * x + y
    cps = []
    for k in range(len(shards)):
        rows = _half_rows(shards[k].shape[0], c)
        for j, (px, py) in enumerate(_other_chips(x, y)):
            slab = (2 * px + py) if as_receiver else me
            cps.append(pltpu.make_async_remote_copy(
                src_ref=shards[k].at[rows], dst_ref=zones[k].at[slab, rows], send_sem=send_sems.at[3 * k + j],
                recv_sem=recv_sems.at[3 * k + j], device_id=(px, py, c), device_id_type=MESH))
    return cps


def _reduce_copies(grads, lands, send_sems, recv_sems):
    x, y, c = _position()
    cps = []
    for k in range(len(grads)):
        for r in range(1, 8):
            px, py, pc = (1 - x if r & 4 else x), (1 - y if r & 2 else y), (1 - c if r & 1 else c)
            cps.append(pltpu.make_async_remote_copy(
                src_ref=grads[k].at[2 * px + py, _half_rows(grads[k].shape[1], pc)], dst_ref=lands[k].at[r - 1],
                send_sem=send_sems.at[7 * k + r - 1], recv_sem=recv_sems.at[7 * k + r - 1],
                device_id=(px, py, pc), device_id_type=MESH))
    return cps


def _split_start(copies_of, srcs, zone_shapes, per, after, name):
    n = len(srcs)

    def body(*refs):
        ins, zones = refs[:n], refs[n:2 * n]
        send_sems, recv_sems = refs[2 * n + 1:2 * n + 3]
        for cp in copies_of(ins, zones, send_sems, recv_sems):
            cp.start()
        refs[-1][...] = jnp.zeros_like(refs[-1])

    hbm = lambda v: pltpu.with_memory_space_constraint(v, pltpu.HBM)
    zones = [lax.empty(sh, srcs[0].dtype) for sh in zone_shapes]
    outs = pl.pallas_call(
        body, name=name,
        out_shape=(pltpu.SemaphoreType.DMA((per * n,)), pltpu.SemaphoreType.DMA((per * n,)),
                   *[pltpu.HBM(v.shape, v.dtype) for v in srcs], *[pltpu.HBM(z.shape, z.dtype) for z in zones],
                   jax.ShapeDtypeStruct((8, LANES), F32)),
        in_specs=[_HBM] * (2 * n) + [pl.BlockSpec(memory_space=pl.ANY)],
        out_specs=(_SEM, _SEM, *[_HBM] * (2 * n), pl.BlockSpec(memory_space=pltpu.VMEM)),
        input_output_aliases={i: 2 + i for i in range(2 * n)},
        compiler_params=pltpu.CompilerParams(has_side_effects=_DATAFLOW),
    )(*[hbm(v) for v in srcs], *[hbm(z) for z in zones], after)
    return outs[0], outs[1], list(outs[2:2 + n]), list(outs[2 + n:2 + 2 * n]), outs[-1]


def _split_wait(copies_of, started, after, name):
    send_sems, recv_sems, srcs, zones, _ = started
    n = len(srcs)

    def body(*refs):
        for cp in copies_of(refs[:n], refs[n:2 * n], refs[2 * n], refs[2 * n + 1]):
            cp.wait_send()
            cp.wait_recv()

    outs = pl.pallas_call(
        body, name=name,
        out_shape=tuple(pltpu.HBM(v.shape, v.dtype) for v in list(srcs) + list(zones)),
        in_specs=[_HBM] * (2 * n) + [_SEM, _SEM, pl.BlockSpec(memory_space=pl.ANY)],
        out_specs=tuple([_HBM] * (2 * n)),
        input_output_aliases={i: i for i in range(2 * n)},
        compiler_params=pltpu.CompilerParams(has_side_effects=_DATAFLOW),
    )(*srcs, *zones, send_sems, recv_sems, after)
    return list(outs[:n]), list(outs[n:])


def gather_start(shards, after, *, name):
    return _split_start(lambda a, z, s, r: _gather_copies(a, z, s, r, False), shards, [(4,) + v.shape for v in shards], 3,
                        after, name)


def gather_wait(started, after, *, name):
    return _split_wait(lambda a, z, s, r: _gather_copies(a, z, s, r, True), started, after, name)


def reduce_start(grads, after, *, name):
    return _split_start(_reduce_copies, grads, [(7, v.shape[1] // 2, v.shape[2]) for v in grads], 7, after, name)


def reduce_wait(started, after, *, name):
    return _split_wait(_reduce_copies, started, after, name)


def pair_fill(shards, zones, *, name):
    n = len(shards)

    def body(*refs):
        ins, zin = refs[:n], refs[2 * n:3 * n]
        send_sems, recv_sems, local_sems = refs[3 * n:]
        x, y, c = _position()
        me = 2 * x + y
        local = [pltpu.make_async_copy(ins[k], zin[k].at[me], local_sems.at[k]) for k in range(n)]
        for cp in local:
            cp.start()
        sends, recvs = [], []
        for k in range(n):
            mine, theirs = _half_rows(ins[k].shape[0], c), _half_rows(ins[k].shape[0], 1 - c)
            for j, (px, py) in enumerate(_other_chips(x, y)):
                mk = lambda rows: pltpu.make_async_remote_copy(
                    src_ref=zin[k].at[2 * px + py, rows], dst_ref=zin[k].at[2 * px + py, rows],
                    send_sem=send_sems.at[3 * k + j], recv_sem=recv_sems.at[3 * k + j],
                    device_id=(x, y, 1 - c), device_id_type=MESH)
                sends.append(mk(mine))
                recvs.append(mk(theirs))
        for cp in sends:
            cp.start()
        for cp in recvs:
            cp.wait_recv()
        for cp in sends:
            cp.wait_send()
        for cp in local:
            cp.wait()

    return pl.pallas_call(
        body, name=name, out_shape=[jax.ShapeDtypeStruct(z.shape, z.dtype) for z in zones],
        in_specs=[_ANY] * (2 * n), out_specs=[_ANY] * n, input_output_aliases={n + i: i for i in range(n)},
        scratch_shapes=[pltpu.SemaphoreType.DMA((3 * n,)), pltpu.SemaphoreType.DMA((3 * n,)),
                        pltpu.SemaphoreType.DMA((n,))],
    )(*shards, *zones)


def sum_pieces(own, others, *, name):
    rows, cols = own.shape
    br = _row_block(rows)

    def body(own_ref, oth_ref, o_ref):
        acc = own_ref[...].astype(F32)
        for i in range(7):
            acc = acc + oth_ref[i].astype(F32)
        o_ref[...] = acc

    return pl.pallas_call(
        body, name=name, grid=(rows // br,),
        in_specs=[pl.BlockSpec((br, cols), lambda i: (i, 0)), pl.BlockSpec((7, br, cols), lambda i: (0, i, 0))],
        out_specs=pl.BlockSpec((br, cols), lambda i: (i, 0)), out_shape=jax.ShapeDtypeStruct((rows, cols), F32),
        compiler_params=_cparams(("parallel",)),
    )(own, others)


def sum_arrays(arrs, out_dtype, *, name):
    rows, cols = arrs[0].shape
    br = _row_block(rows)

    def body(*refs):
        acc = refs[0][...].astype(F32)
        for rf in refs[1:-1]:
            acc = acc + rf[...].astype(F32)
        refs[-1][...] = acc.astype(out_dtype)

    spec = pl.BlockSpec((br, cols), lambda i: (i, 0))
    return pl.pallas_call(
        body, name=name, grid=(rows // br,), in_specs=[spec] * len(arrs), out_specs=spec,
        out_shape=jax.ShapeDtypeStruct((rows, cols), out_dtype),
        compiler_params=_cparams(("parallel",)),
    )(*arrs)


_WEIGHTS = ('c_ctx', 'ada_w', 'ada_b', 'norm_mix_pre', 'norm_mix_post', 'norm_ffn_pre', 'norm_ffn_post', 'w_in',
            'w_out', 'gla_gate_up', 'gla_gate_b', 'gla_norm', 'ssd_conv_w', 'ssd_conv_b', 'ssd_dt_bias', 'ssd_a_log',
            'ssd_d', 'ssd_norm', 'ret_norm', 'ffn_w13', 'ffn_w2')
_BIG = ('ada_w', 'w_in', 'w_out', 'ffn_w13', 'ffn_w2')
_EXCHANGED = ('w_in', 'w_out', 'ffn_w13', 'ffn_w2')

GLA_CFG = ScanCfg("gla", _chunk_vector, GLA_HEADS, GLA_DV, LANES, 128, (None, None, (GLA_HEADS, GLA_DV), None))
SSD_CFG = ScanCfg("ssd", _chunk_shared, SSD_HEADS, SSD_DV, SSD_STATE, 128,
                  ((SSD_GROUPS, SSD_STATE), (SSD_GROUPS, SSD_STATE), (SSD_HEADS, SSD_DV), None, None))
RET_CFG = ScanCfg("ret", _chunk_const, RET_HEADS, RET_DH, RET_DH, 128, ((RET_HEADS, RET_DH),) * 3)


def _permute_cols(w):
    parts = [jnp.zeros((w.shape[0], n), w.dtype) if src is None else w[:, src:src + n] for src, n in _PERM]
    return jnp.concatenate(parts, axis=1)


def _unpermute_cols(dw):
    return jnp.concatenate([dw[:, s:s + n] for s, n in _UNPERM], axis=1)


def _rope_tables(t_ctx, t_lat):
    grid_w = 64
    rows = t_lat // grid_w
    row = np.repeat(np.arange(rows), grid_w).astype(np.float32)
    col = np.tile(np.arange(grid_w), rows).astype(np.float32)
    inv = (np.float32(10000.0) ** (-np.arange(16, dtype=np.float32) / np.float32(16))).astype(np.float32)
    ang = np.concatenate([row[:, None] * inv, col[:, None] * inv], axis=-1).astype(np.float32)
    cos, sin = np.cos(ang), np.sin(ang)
    cos_t = np.tile(np.concatenate([cos, cos], -1), (1, RET_HEADS))
    sin_t = np.tile(np.concatenate([-sin, sin], -1), (1, RET_HEADS))
    w = RET_HEADS * RET_DH
    cos_t = np.concatenate([np.ones((t_ctx, w)), cos_t], 0).astype(np.float32)
    sin_t = np.concatenate([np.zeros((t_ctx, w)), sin_t], 0).astype(np.float32)
    return jnp.asarray(cos_t), jnp.asarray(sin_t)


def _pad_lanes(v):
    v = v.reshape(1, -1)
    return jnp.pad(v, ((0, 0), (0, LANES - v.shape[1])))


def _pack(arrs, rows):
    flat = jnp.concatenate([a.reshape(-1) for a in arrs])
    return jnp.pad(flat, (0, rows * LANES - flat.shape[0])).reshape(rows, LANES)


def _unpack(packed, shapes):
    flat, out, i = packed.reshape(-1), [], 0
    for s in shapes:
        n = int(np.prod(s))
        out.append(flat[i:i + n].reshape(s))
        i += n
    return out


def _rows_for(shapes):
    n = sum(int(np.prod(s)) for s in shapes)
    return -(-n // (8 * LANES)) * 8


def _layer_params(a, l, conv_full):
    row = lambda v: v.reshape(1, -1)
    return dict(
        nmp=row(a['norm_mix_pre'][l]), nmpost=row(a['norm_mix_post'][l]), nfp=row(a['norm_ffn_pre'][l]),
        nfpost=row(a['norm_ffn_post'][l]),
        convw8=jnp.pad(conv_full[l], ((0, 3), (0, 0))), convb=row(a['ssd_conv_b'][l]),
        dtb=_pad_lanes(a['ssd_dt_bias'][l]), alog=_pad_lanes(a['ssd_a_log'][l]),
        up=a['gla_gate_up'][l], gbias=a['gla_gate_b'][l][:, None, :],
        gla_nw=row(a['gla_norm'][l]), dsk=row(jnp.repeat(a['ssd_d'][l], SSD_DV)), ssd_nw=row(a['ssd_norm'][l]),
        ret_nw=row(a['ret_norm'][l]))


def _layer_fwd(xs, mod, w, lp, consts, t_ctx, tag):
    cos, sin, pm = consts
    kw = dict(t_ctx=t_ctx)
    p, h1 = nm_matmul(xs, lp['nmp'], mod, w['w_in'], sh=0, sc=1, out_dtype=F32, name="in_proj", **kw)
    u = conv_fwd(p, lp['convw8'], lp['convb'], name="conv_fwd", **kw)
    act, sdt, sg = ssd_act_fwd(u, p, lp['dtb'], lp['alog'], name="ssd_act_fwd", **kw)
    qs, ggf, ggb = gla_pre_fwd(p, lp['up'], lp['gbias'], name="gla_pre_fwd", **kw)
    rq, rk = rope_pair([(p, 256, P_RQ // 256)], [(p, 256, P_RK // 256)], cos, sin, transpose=False,
                       name="rope_fwd", **kw)
    gla_in = lambda g: [_full(qs), (p, 128, P_GK // 128), (p, 256, P_GV // 256), _full(g)]
    ssd_in = [(act, 256, 3), (act, 256, 2), (act, 512, 0), _full(sg), _full(sdt)]
    ret_in = [_full(rq), _full(rk), (p, 256, P_RV // 256)]
    scans = dict(
        gla_f=(GLA_CFG, gla_in(ggf), False, {}), gla_b=(GLA_CFG, gla_in(ggb), True, {}),
        ssd_f=(SSD_CFG, ssd_in, False, dict(g_off=0)), ssd_b=(SSD_CFG, ssd_in, True, dict(g_off=SSD_HEADS)),
        ret_f=(RET_CFG, ret_in, False, {}), ret_b=(RET_CFG, ret_in, True, {}))
    so = {}
    for key, (cfg, ins, rev, extra) in scans.items():
        so[key] = scan_fwd(cfg, ins, t_ctx=t_ctx, reverse=rev, **extra)
    post_in = (so['gla_f'][0], so['gla_b'][0], so['ssd_f'][0], so['ssd_b'][0], so['ret_f'][0], so['ret_b'][0],
               p, act, lp['gla_nw'], lp['dsk'], lp['ssd_nw'], lp['ret_nw'], pm)
    y = mix_post_fwd(*post_in, name="mix_post_fwd", **kw)
    xs1, zmix = mm_postnorm(y, w['w_out'], xs, mod, lp['nmpost'], gt=2, swiglu=False, name="out_proj", **kw)
    u13, h2 = nm_matmul(xs1, lp['nfp'], mod, w['ffn_w13'], sh=3, sc=4, out_dtype=BF16, name="ffn_up", **kw)
    xs2, zffn, actf = mm_postnorm(u13, w['ffn_w2'], xs1, mod, lp['nfpost'], gt=5, swiglu=True, name="ffn_down", **kw)
    saved = dict(xs=xs, p=p, h1=h1, u=u, scans=scans, states={k: v[1] for k, v in so.items()}, post_in=post_in,
                 y=y, xs1=xs1, zmix=zmix, u13=u13, h2=h2, zffn=zffn, actf=actf)
    return xs2, saved


def _layer_bwd(dxs, sv, mod, w, lp, consts, t_ctx):
    cos, sin, pm = consts
    kw = dict(t_ctx=t_ctx)
    du13, dzb, dgt2, dnfpost = postnorm_bwd(dxs, sv['zffn'], mod, lp['nfpost'], w['ffn_w2'], sv['u13'], gt=5,
                                            name="ffn_down_bwd", **kw)
    dw2 = grad_matmul(sv['actf'], dzb, by_shard=False, name="ffn_w2_grad")
    dxs1, dnfp, dsh2, dsc2 = nm_bwd(du13, w['ffn_w13'], sv['xs1'], lp['nfp'], mod, dxs, sh=3, sc=4,
                                    name="ffn_up_bwd", **kw)
    dw13 = grad_matmul(sv['h2'], du13, by_shard=True, name="ffn_w13_grad")
    dy, dzb1, dgt1, dnmpost = postnorm_bwd(dxs1, sv['zmix'], mod, lp['nmpost'], w['w_out'], None, gt=2,
                                           name="out_proj_bwd", **kw)
    dwout = grad_matmul(sv['y'], dzb1, by_shard=False, name="w_out_grad")
    (dgo, dso, dro, drg, dz, dgg, dxs_skip, dgla_nw, ddsk, dssd_nw, dret_nw) = mix_post_bwd(
        *sv['post_in'], dy, name="mix_post_bwd", **kw)
    douts = dict(gla=_full(dgo), ssd=_full(dso), ret=_full(dro))
    sb = {}
    for key, (cfg, ins, rev, extra) in sv['scans'].items():
        sb[key] = scan_bwd(cfg, ins, sv['states'][key], douts[key[:3]], t_ctx=t_ctx, reverse=rev, **extra)
    p = sv['p']
    drq, drk = rope_pair([_full(sb['ret_f'][0]), _full(sb['ret_b'][0])], [_full(sb['ret_f'][1]), _full(sb['ret_b'][1])],
                         cos, sin, transpose=True, name="rope_bwd", **kw)
    dgq, dlr, dup, dgbias = gla_pre_bwd(p, lp['up'], lp['gbias'], [sb['gla_f'][0], sb['gla_b'][0]],
                                        sb['gla_f'][3], sb['gla_b'][3], name="gla_pre_bwd", **kw)
    du, ddt, ddtb, dalog = ssd_act_bwd(
        sv['u'], p, lp['dtb'], lp['alog'], [sb['ssd_f'][2], sb['ssd_b'][2], dxs_skip], [sb['ssd_f'][1], sb['ssd_b'][1]],
        [sb['ssd_f'][0], sb['ssd_b'][0]], [sb['ssd_f'][4], sb['ssd_b'][4]], [sb['ssd_f'][3], sb['ssd_b'][3]],
        name="ssd_act_bwd", **kw)
    dxbc, dconvw8, dconvb = conv_bwd(p, du, lp['convw8'], name="conv_bwd", **kw)
    dp = dp_assemble([
        (P_XBC, 1024, [dxbc]), (P_RQ, 256, [drq]), (P_RK, 256, [drk]), (P_RV, 256, [sb['ret_f'][2], sb['ret_b'][2]]),
        (P_RG, 256, [dgg]), (P_Z, 512, [dz]), (P_GV, 256, [sb['gla_f'][2], sb['gla_b'][2]]), (P_GR, 256, [drg]),
        (P_GQ, 128, [dgq]), (P_GK, 128, [sb['gla_f'][1], sb['gla_b'][1]]), (P_LR, 128, [dlr]), (P_DT, 128, [ddt])],
        name="dp_assemble", **kw)
    dxs0, dnmp, dsh1, dsc1 = nm_bwd(dp, w['w_in'], sv['xs'], lp['nmp'], mod, dxs1, sh=0, sc=1, name="in_proj_bwd", **kw)
    dwin = grad_matmul(sv['h1'], dp, by_shard=False, name="w_in_grad")
    dmod = jnp.concatenate([dsh1, dsc1, dgt1, dsh2, dsc2, dgt2], axis=1)
    small = dict(
        norm_mix_pre=dnmp, norm_mix_post=dnmpost, norm_ffn_pre=dnfp, norm_ffn_post=dnfpost,
        gla_gate_up=dup, gla_gate_b=dgbias[:, 0, :], gla_norm=dgla_nw,
        ssd_conv_w=dconvw8[0:5], ssd_conv_b=dconvb, ssd_dt_bias=ddtb[0, 0:16].reshape(2, SSD_HEADS),
        ssd_a_log=dalog[0, 0:16].reshape(2, SSD_HEADS), ssd_d=ddsk.reshape(SSD_HEADS, SSD_DV).sum(-1),
        ssd_norm=dssd_nw, ret_norm=dret_nw)
    big = dict(w_in=dwin, w_out=dwout, ffn_w13=dw13, ffn_w2=dw2)
    return dxs0, big, dmod, small


def _take_chips(g, m_per):
    return g.reshape(8, m_per, g.shape[1])[0::2]


def kernel(x, c, ctx, c_ctx, ada_w, ada_b, norm_mix_pre, norm_mix_post, norm_ffn_pre, norm_ffn_post, w_in, w_out, gla_gate_up, gla_gate_b, gla_norm, ssd_conv_w, ssd_conv_b, ssd_dt_bias, ssd_a_log, ssd_d, ssd_norm, ret_norm, ffn_w13, ffn_w2, loss_target, m_c_ctx, m_ada_w, m_ada_b, m_norm_mix_pre, m_norm_mix_post, m_norm_ffn_pre, m_norm_ffn_post, m_w_in, m_w_out, m_gla_gate_up, m_gla_gate_b, m_gla_norm, m_ssd_conv_w, m_ssd_conv_b, m_ssd_dt_bias, m_ssd_a_log, m_ssd_d, m_ssd_norm, m_ret_norm, m_ffn_w13, m_ffn_w2, v_c_ctx, v_ada_w, v_ada_b, v_norm_mix_pre, v_norm_mix_post, v_norm_ffn_pre, v_norm_ffn_post, v_w_in, v_w_out, v_gla_gate_up, v_gla_gate_b, v_gla_norm, v_ssd_conv_w, v_ssd_conv_b, v_ssd_dt_bias, v_ssd_a_log, v_ssd_d, v_ssd_norm, v_ret_norm, v_ffn_w13, v_ffn_w2):
    a = dict(locals())
    depth = ada_w.shape[0]
    t_ctx, t_lat = ctx.shape[1], x.shape[1]
    xi, yi, ci = _position()
    dev, chip = 4 * xi + 2 * yi + ci, 2 * xi + yi
    ncol = ada_w.shape[2]

    cw = ssd_conv_w.reshape(-1)
    blk = jnp.concatenate([c[0], cw, jnp.zeros((7 * D - cw.shape[0],), F32)]).reshape(8, D)
    g0 = all_gather8(blk, name="gather_cond").reshape(8, 8, D)
    conv_full = g0[0::2, 1:8].reshape(4, 7 * D)[:, :cw.shape[0]].reshape(4, depth, 5, D // 4)
    conv_full = conv_full.transpose(1, 2, 0, 3).reshape(depth, 5, D)
    cg = jnp.concatenate([g0[:, 0], jnp.broadcast_to(c_ctx[None], (8, D))], axis=0)
    part = ada_fwd(cg, ada_w, name="ada_fwd")
    g1 = _take_chips(all_gather8(part.reshape(depth * 16, ncol), name="gather_mod"), depth * 16)
    mod_all = g1.reshape(4, depth, 16, ncol).transpose(1, 2, 0, 3).reshape(depth, 16, 4 * ncol) + ada_b[:, None, :]
    mods = [jnp.stack([mod_all[l, 8], lax.dynamic_index_in_dim(mod_all[l], dev, 0, keepdims=False)]).reshape(2, 6, D)
            for l in range(depth)]

    def start_gather(l, after):
        return gather_start([a[n][l].astype(BF16) for n in _EXCHANGED], after, name="weight_gather_start")

    def finish_gather(started, after):
        shards, zones = gather_wait(started, after, name="weight_gather_wait")
        sh = dict(zip(_EXCHANGED, pair_fill(shards, zones, name="weight_pair")))
        cols = lambda v: v.transpose(1, 0, 2).reshape(v.shape[1], 4 * v.shape[2])
        rows = lambda v: v.reshape(4 * v.shape[1], v.shape[2])
        return dict(w_in=_permute_cols(cols(sh['w_in'])), w_out=rows(sh['w_out']), ffn_w13=cols(sh['ffn_w13']),
                    ffn_w2=rows(sh['ffn_w2']))

    cos, sin = _rope_tables(t_ctx, t_lat)
    pm = np.kron(np.eye(RET_HEADS), np.full((RET_DH, RET_DH), 1.0 / RET_DH)).astype(np.float32)
    consts = (cos, sin, jnp.asarray(pm))

    xs = jnp.concatenate([ctx[0], x[0]], axis=0)
    saved, lws, lps = [], [], []
    started = start_gather(0, mod_all)
    for l in range(depth):
        lws.append(finish_gather(started, xs))
        mod = mods[l]
        if l + 1 < depth:
            started = start_gather(l + 1, lws[l]['w_out'])
            mod = mod + started[4][0, 0]
        lps.append(_layer_params(a, l, conv_full))
        xs, sv = _layer_fwd(xs, mod, lws[l], lps[l], consts, t_ctx, l)
        saved.append(sv)
    lvec, dxs = loss_head(xs, loss_target[0], t_ctx=t_ctx, name="loss_head")
    loss = lax.psum(lvec[0, 0], ("x", "y", "c"))

    def by_shard(n, g):
        if n == 'w_in':
            g = _unpermute_cols(g)
            return g.reshape(g.shape[0], 4, g.shape[1] // 4).transpose(1, 0, 2)
        return g if g.ndim == 3 else g.reshape(4, g.shape[0] // 4, g.shape[1])

    small = [None] * depth
    dmods = [None] * depth
    flying, landed = None, [None] * depth
    for l in reversed(range(depth)):
        mod = mods[l] if flying is None else mods[l] + flying[1][4][0, 0]
        dxs, bg, dmods[l], small[l] = _layer_bwd(dxs, saved[l], mod, lws[l], lps[l], consts, t_ctx)
        parts = [by_shard(n, bg[n]) for n in _EXCHANGED]
        after = dxs
        if flying is not None:
            landed[flying[0]] = reduce_wait(flying[1], dxs, name="grad_exchange_wait")
            after = landed[flying[0]][1][0]
        flying = (l, reduce_start(parts, after, name="grad_exchange_start"))
    grad_x = dxs[t_ctx:][None]

    grads = {}
    small_names = [n for n in _WEIGHTS if n not in _BIG and n not in ('c_ctx', 'ada_b')]
    small_shapes = [((depth, 5, D) if n == 'ssd_conv_w' else a[n].shape) for n in small_names]
    srows = _rows_for(small_shapes)
    dm = jnp.stack(dmods).reshape(depth, 2, 6 * D)
    mrows = dm.size // LANES
    vec = jnp.concatenate([_pack([jnp.stack([small[l][n] for l in range(depth)]) for n in small_names], srows),
                           dm.reshape(mrows, LANES)], axis=0)
    g2 = all_gather8(vec, name="gather_small").reshape(8, srows + mrows, LANES)
    small_sum = sum_leading(g2[:, :srows], F32, name="small_sum")
    sg = dict(zip(small_names, _unpack(small_sum, small_shapes)))
    dm_all = g2[:, srows:].reshape(8, depth, 2, 6 * D)
    grads['ada_b'] = sum_leading(dm_all.transpose(0, 2, 1, 3).reshape(16, depth * 6 * D // LANES, LANES), F32,
                                 name="ada_b_sum").reshape(depth, 6 * D)
    dm_cols = lax.dynamic_slice_in_dim(dm_all, chip * ncol, ncol, axis=3)
    dmod16 = jnp.concatenate([dm_cols[:, :, 1].transpose(1, 0, 2), dm_cols[:, :, 0].transpose(1, 0, 2)], axis=1)
    grads['ada_w'], dcg = ada_bwd(cg, ada_w, dmod16, name="ada_bwd")
    dcc = sum_leading(dcg[8:16].reshape(8, 1, D), F32, name="c_ctx_rows_sum")
    g3 = all_gather8(jnp.zeros((8, D), F32).at[0:1].set(dcc), name="gather_c_ctx").reshape(8, 8, D)
    grads['c_ctx'] = sum_leading(g3[0::2, 0:1], F32, name="c_ctx_sum").reshape(D)
    for n in small_names:
        grads[n] = sg[n]
    conv_grad_shard = lax.dynamic_slice_in_dim(sg['ssd_conv_w'], chip * (D // 4), D // 4, axis=2)
    grads['ssd_conv_w'] = conv_grad_shard

    landed[flying[0]] = reduce_wait(flying[1], grads['c_ctx'], name="grad_exchange_wait")
    mine_sum = []
    for l in range(depth):
        parts, zones = landed[l]
        for p_, z_ in zip(parts, zones):
            own = lax.dynamic_slice_in_dim(lax.dynamic_index_in_dim(p_, chip, 0, keepdims=False),
                                           (p_.shape[1] // 2) * ci, p_.shape[1] // 2, axis=0)
            mine_sum.append(sum_pieces(own, z_, name="grad_piece_sum"))
    sib_sum = pair_exchange(mine_sum, name="grad_pair_back")
    for k, n in enumerate(_EXCHANGED):
        per_layer = [jnp.where(ci == 0, jnp.concatenate([mine_sum[len(_EXCHANGED) * l + k], sib_sum[len(_EXCHANGED) * l + k]]),
                               jnp.concatenate([sib_sum[len(_EXCHANGED) * l + k], mine_sum[len(_EXCHANGED) * l + k]]))
                     for l in range(depth)]
        grads[n] = jnp.stack(per_layer)

    delta, new_m, new_v = {}, {}, {}
    for n in _BIG:
        sh = a[n].shape
        two_d = lambda v: v.reshape(-1, sh[-1])
        d_, m_, v_ = adamw(two_d(a[n]), two_d(grads[n]), two_d(a['m_' + n]), two_d(a['v_' + n]), name="adamw_" + n)
        delta[n], new_m[n], new_v[n] = d_.reshape(sh), m_.reshape(sh), v_.reshape(sh)
    packed_names = [n for n in _WEIGHTS if n not in _BIG]
    shapes = [a[n].shape for n in packed_names]
    prow = _rows_for(shapes)
    pk = lambda pre: _pack([(grads[n] if pre == 'g' else a[pre + n]) for n in packed_names], prow)
    d_, m_, v_ = adamw(pk(''), pk('g'), pk('m_'), pk('v_'), name="adamw_small")
    for n, dd, mm, vv in zip(packed_names, _unpack(d_, shapes), _unpack(m_, shapes), _unpack(v_, shapes)):
        delta[n], new_m[n], new_v[n] = dd, mm, vv

    return (loss, grad_x, *[grads[n] for n in _WEIGHTS], *[delta[n] for n in _WEIGHTS],
            *[new_m[n] for n in _WEIGHTS], *[new_v[n] for n in _WEIGHTS])
```
